```python
import math
import jax, jax.numpy as jnp
from jax import lax
import numpy as np

D_MODEL = 1024
BATCH = 8
SEQ = 2048
DEPTH = 4
DEC_BATCH = 128
DEC_SEQ = 1
PAST_LEN = 16384
PAGE_SIZE = 128

HEAD_DIM = 64
A_WIDTH = 3 * D_MODEL // 8
B_WIDTH = 3 * D_MODEL // 8
C_WIDTH = D_MODEL - A_WIDTH - B_WIDTH
H_A = A_WIDTH // HEAD_DIM
H_B = B_WIDTH // HEAD_DIM
H_C = C_WIDTH // HEAD_DIM
CONV_W = 4
CHUNK = 64
W_LORA = 64
A_LORA = 64
G_LORA = 128
V_LORA = 32
A_COLS = 4 * A_WIDTH + 2 * H_A
B_COLS = 3 * B_WIDTH + W_LORA + A_LORA + G_LORA
C_COLS = 4 * C_WIDTH
IN_COLS = A_COLS + B_COLS + C_COLS
D_FF = -(-8 * D_MODEL // (3 * 256)) * 256
RMS_EPS = 1e-6
RWKV_LN_EPS = 64e-5

kernel_name = 'hybrid_gdn_rwkv7_hgrn2_decode_step'


def rmsnorm(x, w):
    xf = x.astype(jnp.float32)
    y = xf * lax.rsqrt(jnp.mean(xf * xf, -1, keepdims=True) + RMS_EPS)
    return (y * w.astype(jnp.float32)).astype(x.dtype)


def gated_rmsnorm(o, w, z):
    return o * lax.rsqrt(jnp.mean(o * o, -1, keepdims=True) + RMS_EPS) * w.astype(jnp.float32) * jax.nn.silu(z)


def l2norm(x):
    return x * lax.rsqrt(jnp.sum(x * x, -1, keepdims=True) + 1e-6)


def split_heads(x, n):
    return x.reshape(x.shape[:-1] + (n, x.shape[-1] // n))


def merge_heads(x):
    return x.reshape(x.shape[:-2] + (x.shape[-2] * x.shape[-1],))


def masked_exp(mask, d):
    return jnp.where(mask, jnp.exp(jnp.where(mask, d, 0.0)), 0.0)


def causal_conv(x, buf, w):
    t = x.shape[1]
    xc = jnp.concatenate([buf, x], axis=1)
    y = xc[:, 0:t] * w[0]
    for j in range(1, CONV_W):
        y = y + xc[:, j:j + t] * w[j]
    return y, xc[:, t:]


def to_chunks(x, c):
    b, t = x.shape[:2]
    n = -(-t // c)
    x = jnp.pad(x, [(0, 0), (0, n * c - t)] + [(0, 0)] * (x.ndim - 2))
    x = x.reshape((b, n, c) + x.shape[2:])
    return jnp.swapaxes(jnp.moveaxis(x, 1, 0), 2, 3)


def from_chunks(o, t):
    o = jnp.moveaxis(jnp.swapaxes(o, 2, 3), 0, 1)
    return o.reshape((o.shape[0], o.shape[1] * o.shape[2]) + o.shape[3:])[:, :t]


def gdn_chunked(q, k, v, g, beta, s0):
    t = q.shape[1]
    c = min(CHUNK, t)
    dv = v.shape[-1]
    qc, kc, vc, gc, bc = (to_chunks(a, c) for a in (q, k, v, g, beta))
    gcum = jnp.cumsum(gc, axis=-1)
    causal = jnp.tril(jnp.ones((c, c), bool))
    strict = jnp.tril(jnp.ones((c, c), bool), -1)
    decay_in = masked_exp(causal, gcum[..., :, None] - gcum[..., None, :])
    m = jnp.where(strict, bc[..., :, None] * jnp.einsum('nbhid,nbhjd->nbhij', kc, kc) * decay_in, 0.0)
    a_mat = m + jnp.eye(c, dtype=m.dtype)
    rhs = jnp.concatenate([vc * bc[..., None], kc * (bc * jnp.exp(gcum))[..., None]], axis=-1)
    sol = lax.linalg.triangular_solve(a_mat, rhs, left_side=True, lower=True, unit_diagonal=True)
    u, w = sol[..., :dv], sol[..., dv:]
    qk = jnp.einsum('nbhid,nbhjd->nbhij', qc, kc) * decay_in
    q_dec = qc * jnp.exp(gcum)[..., None]
    k_dec = kc * jnp.exp(gcum[..., -1:] - gcum)[..., None]
    g_last = jnp.exp(gcum[..., -1])

    def step(s, xs):
        u_i, w_i, qk_i, qd_i, kd_i, gl_i = xs
        v_new = u_i - jnp.einsum('bhck,bhkv->bhcv', w_i, s)
        o = jnp.einsum('bhck,bhkv->bhcv', qd_i, s) + jnp.einsum('bhij,bhjv->bhiv', qk_i, v_new)
        s = s * gl_i[..., None, None] + jnp.einsum('bhck,bhcv->bhkv', kd_i, v_new)
        return s, o

    s, o = lax.scan(step, s0, (u, w, qk, q_dec, k_dec, g_last))
    return from_chunks(o, t), s


def hgrn_chunked(q, k, i, log_f, s0):
    t = q.shape[1]
    c = min(CHUNK, t)
    qc, kc, ic, fc = (to_chunks(a, c) for a in (q, k, i, log_f))
    bcum = jnp.cumsum(fc, axis=-2)
    causal = jnp.tril(jnp.ones((c, c), bool))[..., None]

    def step(s, xs):
        q_i, k_i, v_i, b_i = xs
        dec = masked_exp(causal, b_i[..., :, None, :] - b_i[..., None, :, :])
        att = jnp.einsum('bhid,bhjd,bhijd->bhij', q_i, k_i, dec)
        o = jnp.einsum('bhid,bhdv->bhiv', q_i * jnp.exp(b_i), s) + jnp.einsum('bhij,bhjv->bhiv', att, v_i)
        bl = b_i[..., -1, :]
        s = jnp.exp(bl)[..., None] * s + jnp.einsum('bhcd,bhcv->bhdv', k_i * jnp.exp(bl[..., None, :] - b_i), v_i)
        return s, o

    s, o = lax.scan(step, s0, (qc, kc, ic, bcum))
    return from_chunks(o, t), s


def rwkv7_scan(r, decay, k, v, kk, a, s0):
    def step(s, xs):
        r_t, w_t, k_t, v_t, kk_t, a_t = xs
        sa = jnp.einsum('bhvk,bhk->bhv', s, -kk_t)
        s = s * w_t[:, :, None, :] + sa[..., None] * (kk_t * a_t)[:, :, None, :] + v_t[..., None] * k_t[:, :, None, :]
        return s, jnp.einsum('bhvk,bhk->bhv', s, r_t)

    xs = tuple(jnp.moveaxis(a_, 1, 0) for a_ in (r, decay, k, v, kk, a))
    s, y = lax.scan(step, s0, xs)
    return jnp.moveaxis(y, 0, 1), s


def gdn_mixer(p, s0, conv_buf, P, l):
    qkv, z, b, a = jnp.split(p, [3 * A_WIDTH, 4 * A_WIDTH, 4 * A_WIDTH + H_A], axis=-1)
    qkv, new_buf = causal_conv(qkv, conv_buf.astype(jnp.float32), P['gdn_conv_w'][l].astype(jnp.float32))
    qkv = jax.nn.silu(qkv)
    q, k, v = (split_heads(t_, H_A) for t_ in jnp.split(qkv, 3, axis=-1))
    q = l2norm(q) * HEAD_DIM ** -0.5
    k = l2norm(k)
    beta = jax.nn.sigmoid(b)
    g = -jnp.exp(P['gdn_a_log'][l].astype(jnp.float32)) * jax.nn.softplus(a + P['gdn_dt_bias'][l])
    o, s = gdn_chunked(q, k, v, g, beta, s0.astype(jnp.float32))
    o = gated_rmsnorm(o, P['gdn_norm_w'][l], split_heads(z, H_A))
    return merge_heads(o), s, new_buf


def rwkv_mixer(p, s0, shift_buf, v_first, P, l):
    prev = jnp.concatenate([shift_buf.astype(jnp.float32)[:, None], p[:, :-1]], axis=1)
    xs = p + (prev - p) * P['rwkv_mu'][l]
    new_shift = p[:, -1]
    r, k, v, wd, ad, gd = jnp.split(
        xs, [B_WIDTH, 2 * B_WIDTH, 3 * B_WIDTH, 3 * B_WIDTH + W_LORA, 3 * B_WIDTH + W_LORA + A_LORA], axis=-1)
    w_log = -jax.nn.softplus(-(P['rwkv_w0'][l] + jnp.einsum('btr,rc->btc', jnp.tanh(wd), P['rwkv_w2'][l]))) - 0.5
    decay = jnp.exp(-jnp.exp(w_log))
    a = jax.nn.sigmoid(P['rwkv_a0'][l] + jnp.einsum('btr,rc->btc', ad, P['rwkv_a2'][l]))
    g = jnp.einsum('btr,rc->btc', jax.nn.sigmoid(gd), P['rwkv_g2'][l])
    if l == 0:
        v_first = v
    else:
        mixv = jax.nn.sigmoid(P['rwkv_v0'][l - 1] + jnp.einsum(
            'btr,rc->btc', jnp.einsum('btc,cr->btr', v, P['rwkv_v1'][l - 1]), P['rwkv_v2'][l - 1]))
        v = v + (v_first - v) * mixv
    kk = l2norm(split_heads(k * P['rwkv_k_k'][l], H_B))
    k = k * (1.0 + (a - 1.0) * P['rwkv_k_a'][l])
    rh, kh, vh, ah, dh = (split_heads(t_, H_B) for t_ in (r, k, v, a, decay))
    y, s = rwkv7_scan(rh, dh, kh, vh, kk, ah, s0.astype(jnp.float32))
    mu = jnp.mean(y, -1, keepdims=True)
    yc = y - mu
    y = yc * lax.rsqrt(jnp.mean(yc * yc, -1, keepdims=True) + RWKV_LN_EPS)
    y = merge_heads(y) * P['rwkv_ln_w'][l] + P['rwkv_ln_b'][l]
    bonus = jnp.sum(rh * kh * P['rwkv_r_k'][l], -1, keepdims=True) * vh
    y = (y + merge_heads(bonus)) * g
    return y, s, new_shift, v_first


def hgrn_mixer(p, s0, lb, P, l):
    q, f, i, g = jnp.split(p, 4, axis=-1)
    q = jax.nn.silu(q) * HEAD_DIM ** -0.5
    k = (1.0 - lb) * jax.nn.sigmoid(-f)
    log_f = jnp.log1p(-k)
    o, s = hgrn_chunked(split_heads(q, H_C), split_heads(k, H_C), split_heads(i, H_C),
                        split_heads(log_f, H_C), s0.astype(jnp.float32))
    o = gated_rmsnorm(o, P['hgrn_norm_w'][l], split_heads(g, H_C))
    return merge_heads(o), s


def trunk(x, states, P, lb):
    st_gdn, st_conv, st_rwkv, st_shift, st_hgrn = states
    new = ([], [], [], [], [])
    v_first = None
    for l in range(DEPTH):
        h = rmsnorm(x, P['norm_mix'][l])
        p = jnp.einsum('btd,de->bte', h, P['w_in'][l]).astype(jnp.float32)
        p_a = p[..., :A_COLS]
        p_b = p[..., A_COLS:A_COLS + B_COLS]
        p_c = p[..., A_COLS + B_COLS:]
        y_a, s_a, c_a = gdn_mixer(p_a, st_gdn[l], st_conv[l], P, l)
        y_b, s_b, sh_b, v_first = rwkv_mixer(p_b, st_rwkv[l], st_shift[l], v_first, P, l)
        y_c, s_c = hgrn_mixer(p_c, st_hgrn[l], lb[l], P, l)
        mix = jnp.concatenate([y_a, y_b, y_c], axis=-1).astype(x.dtype)
        x = x + jnp.einsum('bte,ed->btd', mix, P['w_out'][l])
        h = rmsnorm(x, P['norm_ffn'][l])
        u = jax.nn.silu(jnp.einsum('btd,df->btf', h, P['w_gate'][l])) * jnp.einsum('btd,df->btf', h, P['w_up'][l])
        x = x + jnp.einsum('btf,fd->btd', u, P['w_down'][l])
        for lst, s in zip(new, (s_a, c_a, s_b, sh_b, s_c)):
            lst.append(s)
    y = rmsnorm(x, P['norm_final'])
    return y, tuple(jnp.stack(lst).astype(x.dtype) for lst in new)


def setup_inputs(seed: int = 0) -> dict:
    key = jax.random.key(seed)
    ks = jax.random.split(key, 40)
    cnt = iter(range(40))

    def nrm(shape, s):
        return jax.random.normal(ks[next(cnt)], shape, jnp.float32) * s

    def uni(shape, lo, hi):
        return jax.random.uniform(ks[next(cnt)], shape, jnp.float32, lo, hi)

    L = DEPTH
    dt = jnp.exp(uni((L, H_A), math.log(1e-3), math.log(1e-1)))
    return {
        'x_prompt': nrm((BATCH, SEQ, D_MODEL), 1.0),
        'x_sample': nrm((DEC_BATCH, DEC_SEQ, D_MODEL), 1.0),
        'state_gdn': nrm((L, DEC_BATCH, H_A, HEAD_DIM, HEAD_DIM), 0.3),
        'state_gdn_conv': nrm((L, DEC_BATCH, CONV_W - 1, 3 * A_WIDTH), 1.0),
        'state_rwkv': nrm((L, DEC_BATCH, H_B, HEAD_DIM, HEAD_DIM), 0.3),
        'state_rwkv_shift': nrm((L, DEC_BATCH, B_COLS), 1.0),
        'state_hgrn': nrm((L, DEC_BATCH, H_C, HEAD_DIM, HEAD_DIM), 0.3),
        'norm_mix': 1.0 + nrm((L, D_MODEL), 0.02),
        'w_in': nrm((L, D_MODEL, IN_COLS), D_MODEL ** -0.5),
        'w_out': nrm((L, D_MODEL, D_MODEL), 0.5 * D_MODEL ** -0.5),
        'gdn_conv_w': nrm((L, CONV_W, 3 * A_WIDTH), CONV_W ** -0.5),
        'gdn_a_log': jnp.log(uni((L, H_A), 1.0, 16.0)),
        'gdn_dt_bias': dt + jnp.log(-jnp.expm1(-dt)),
        'gdn_norm_w': 1.0 + nrm((L, HEAD_DIM), 0.02),
        'rwkv_mu': uni((L, B_COLS), 0.0, 1.0),
        'rwkv_w0': uni((L, B_WIDTH), -6.0, -1.0),
        'rwkv_w2': nrm((L, W_LORA, B_WIDTH), 0.1),
        'rwkv_a0': nrm((L, B_WIDTH), 0.1),
        'rwkv_a2': nrm((L, A_LORA, B_WIDTH), 0.1),
        'rwkv_g2': nrm((L, G_LORA, B_WIDTH), G_LORA ** -0.5),
        'rwkv_v0': 1.0 + nrm((L - 1, B_WIDTH), 0.1),
        'rwkv_v1': nrm((L - 1, B_WIDTH, V_LORA), B_WIDTH ** -0.5),
        'rwkv_v2': nrm((L - 1, V_LORA, B_WIDTH), 0.1),
        'rwkv_k_k': 0.85 + nrm((L, B_WIDTH), 0.02),
        'rwkv_k_a': 1.0 + nrm((L, B_WIDTH), 0.02),
        'rwkv_r_k': nrm((L, H_B, HEAD_DIM), 0.1),
        'rwkv_ln_w': 1.0 + nrm((L, B_WIDTH), 0.02),
        'rwkv_ln_b': nrm((L, B_WIDTH), 0.01),
        'hgrn_lower_bounds': nrm((L, C_WIDTH), 1.0),
        'hgrn_norm_w': 1.0 + nrm((L, HEAD_DIM), 0.02),
        'norm_ffn': 1.0 + nrm((L, D_MODEL), 0.02),
        'w_gate': nrm((L, D_MODEL, D_FF), D_MODEL ** -0.5),
        'w_up': nrm((L, D_MODEL, D_FF), D_MODEL ** -0.5),
        'w_down': nrm((L, D_FF, D_MODEL), 0.5 * D_FF ** -0.5),
        'norm_final': 1.0 + nrm((D_MODEL,), 0.02),
    }


def reference(x_prompt, x_sample, state_gdn, state_gdn_conv, state_rwkv, state_rwkv_shift, state_hgrn,
              norm_mix, w_in, w_out, gdn_conv_w, gdn_a_log, gdn_dt_bias, gdn_norm_w,
              rwkv_mu, rwkv_w0, rwkv_w2, rwkv_a0, rwkv_a2, rwkv_g2, rwkv_v0, rwkv_v1, rwkv_v2,
              rwkv_k_k, rwkv_k_a, rwkv_r_k, rwkv_ln_w, rwkv_ln_b,
              hgrn_lower_bounds, hgrn_norm_w, norm_ffn, w_gate, w_up, w_down, norm_final):
    P = {
        'norm_mix': norm_mix, 'w_in': w_in, 'w_out': w_out,
        'gdn_conv_w': gdn_conv_w, 'gdn_a_log': gdn_a_log, 'gdn_dt_bias': gdn_dt_bias, 'gdn_norm_w': gdn_norm_w,
        'rwkv_mu': rwkv_mu, 'rwkv_w0': rwkv_w0, 'rwkv_w2': rwkv_w2, 'rwkv_a0': rwkv_a0, 'rwkv_a2': rwkv_a2,
        'rwkv_g2': rwkv_g2, 'rwkv_v0': rwkv_v0, 'rwkv_v1': rwkv_v1, 'rwkv_v2': rwkv_v2,
        'rwkv_k_k': rwkv_k_k, 'rwkv_k_a': rwkv_k_a, 'rwkv_r_k': rwkv_r_k,
        'rwkv_ln_w': rwkv_ln_w, 'rwkv_ln_b': rwkv_ln_b,
        'hgrn_norm_w': hgrn_norm_w, 'norm_ffn': norm_ffn,
        'w_gate': w_gate, 'w_up': w_up, 'w_down': w_down, 'norm_final': norm_final,
    }
    sm = jax.nn.softmax(hgrn_lower_bounds.astype(jnp.float32), axis=0)
    lb = jnp.cumsum(sm, axis=0) - sm[0:1]

    b = x_prompt.shape[0]
    init_prompt = (
        jnp.zeros((DEPTH, b, H_A, HEAD_DIM, HEAD_DIM), jnp.float32),
        jnp.zeros((DEPTH, b, CONV_W - 1, 3 * A_WIDTH), jnp.float32),
        jnp.zeros((DEPTH, b, H_B, HEAD_DIM, HEAD_DIM), jnp.float32),
        jnp.zeros((DEPTH, b, B_COLS), jnp.float32),
        jnp.zeros((DEPTH, b, H_C, HEAD_DIM, HEAD_DIM), jnp.float32),
    )
    y_prompt, (gdn_p, conv_p, rwkv_p, shift_p, hgrn_p) = trunk(x_prompt, init_prompt, P, lb)
    y_sample, (gdn_s, conv_s, rwkv_s, shift_s, hgrn_s) = trunk(
        x_sample, (state_gdn, state_gdn_conv, state_rwkv, state_rwkv_shift, state_hgrn), P, lb)
    return (y_prompt, y_sample, gdn_p, gdn_s, conv_p, conv_s, rwkv_p, rwkv_s, shift_p, shift_s, hgrn_p, hgrn_s)
```

```python
import functools

import jax
import jax.numpy as jnp
from jax import lax
from jax.experimental import pallas as pl
from jax.experimental.pallas import tpu as pltpu

D_MODEL = 1024
DEPTH = 4
HEAD_DIM = 64
A_WIDTH = 384
B_WIDTH = 384
C_WIDTH = 256
H_A = A_WIDTH // HEAD_DIM
H_B = B_WIDTH // HEAD_DIM
H_C = C_WIDTH // HEAD_DIM
CONV_W = 4
W_LORA = 64
A_LORA = 64
G_LORA = 128
V_LORA = 32
A_COLS = 4 * A_WIDTH + 2 * H_A
B_COLS = 3 * B_WIDTH + W_LORA + A_LORA + G_LORA
C_COLS = 4 * C_WIDTH
D_FF = 2816
RMS_EPS = 1e-6
RWKV_LN_EPS = 64e-5

LANES = 128
OFF_QKV = 0
OFF_Z = 3 * A_WIDTH
OFF_B = 4 * A_WIDTH
OFF_C = OFF_B + B_COLS
OFF_BA = OFF_C + C_COLS
P_COLS = OFF_BA + LANES
GATE_LANE_BETA = 0
GATE_LANE_A = H_A

PROMPT_CHUNK = 64
DECODE_CHUNK = 8
CARRY_ROWS = 8

BF = jnp.bfloat16
F32 = jnp.float32


def _mm(a, b):
    return jnp.dot(a.astype(BF), b.astype(BF), preferred_element_type=F32)


def _mm_nt(a, b):
    return lax.dot_general(a.astype(BF), b.astype(BF), (((1,), (1,)), ((), ())),
                           preferred_element_type=F32)


def _mm_tn(a, b):
    return lax.dot_general(a.astype(BF), b.astype(BF), (((0,), (0,)), ((), ())),
                           preferred_element_type=F32)


def _split3(x):
    x1 = x.astype(BF)
    r1 = x - x1.astype(F32)
    x2 = r1.astype(BF)
    x3 = (r1 - x2.astype(F32)).astype(BF)
    return x1, x2, x3


def _mm_sel(sel, x):
    s = sel.astype(BF)
    x1, x2, x3 = _split3(x)
    return (jnp.dot(s, x1, preferred_element_type=F32) + jnp.dot(s, x2, preferred_element_type=F32)
            + jnp.dot(s, x3, preferred_element_type=F32))


def _colsum_bcast(x):
    ones = jnp.ones((x.shape[0], LANES), BF)
    x1, x2, x3 = _split3(x)
    dn = (((0,), (0,)), ((), ()))
    return (lax.dot_general(x1, ones, dn, preferred_element_type=F32)
            + lax.dot_general(x2, ones, dn, preferred_element_type=F32)
            + lax.dot_general(x3, ones, dn, preferred_element_type=F32))


def _sigmoid(x):
    return 1.0 / (1.0 + jnp.exp(-x))


def _silu(x):
    return x * _sigmoid(x)


def _softplus(x):
    return jnp.maximum(x, 0.0) + jnp.log1p(jnp.exp(-jnp.abs(x)))


def _rmsnorm(x, w):
    return x * lax.rsqrt(jnp.mean(x * x, -1, keepdims=True) + RMS_EPS) * w


def _unit_lower_inverse_offdiag(p, n_rows):
    x = p
    q = p
    span = 2
    while span < n_rows:
        q = _mm(q, q)
        x = x + q + _mm(x, q)
        span *= 2
    return x


def _head(x, h):
    return x[:, h * HEAD_DIM:(h + 1) * HEAD_DIM]


def _inproj_kernel(x_ref, nw_ref, w_ref, o_ref):
    h = _rmsnorm(x_ref[...], nw_ref[...]).astype(BF)
    step = 512
    for c in range(0, P_COLS, step):
        o_ref[:, c:c + step] = jnp.dot(h, w_ref[:, c:c + step], preferred_element_type=F32)


def _inproj(x, norm_w, w):
    n = x.shape[0]
    tm = min(512, n)
    assert n % tm == 0
    return pl.pallas_call(
        _inproj_kernel,
        grid=(n // tm,),
        in_specs=[
            pl.BlockSpec((tm, D_MODEL), lambda i: (i, 0)),
            pl.BlockSpec((1, D_MODEL), lambda i: (0, 0)),
            pl.BlockSpec((D_MODEL, P_COLS), lambda i: (0, 0)),
        ],
        out_specs=pl.BlockSpec((tm, P_COLS), lambda i: (i, 0)),
        out_shape=jax.ShapeDtypeStruct((n, P_COLS), F32),
        compiler_params=pltpu.CompilerParams(dimension_semantics=("arbitrary",)),
        name="inproj",
    )(x, norm_w, w)


def _ffn_kernel(final, x_ref, mix_ref, wo_ref, nf_ref, wg_ref, wu_ref, wd_ref, nfin_ref, o_ref):
    x = x_ref[...] + jnp.dot(mix_ref[...], wo_ref[...], preferred_element_type=F32)
    h = _rmsnorm(x, nf_ref[...]).astype(BF)
    half = D_FF // 2
    acc = x
    for c in range(0, D_FF, half):
        gate = jnp.dot(h, wg_ref[:, c:c + half], preferred_element_type=F32)
        up = jnp.dot(h, wu_ref[:, c:c + half], preferred_element_type=F32)
        u = (_silu(gate) * up).astype(BF)
        acc = acc + jnp.dot(u, wd_ref[c:c + half, :], preferred_element_type=F32)
    if final:
        acc = _rmsnorm(acc, nfin_ref[...])
    o_ref[...] = acc


def _ffn(x, mix, w_out, norm_ffn, w_gate, w_up, w_down, norm_final, final):
    n = x.shape[0]
    tm = min(256, n)
    assert n % tm == 0
    const = lambda i: (0, 0)
    return pl.pallas_call(
        functools.partial(_ffn_kernel, final),
        grid=(n // tm,),
        in_specs=[
            pl.BlockSpec((tm, D_MODEL), lambda i: (i, 0)),
            pl.BlockSpec((tm, D_MODEL), lambda i: (i, 0)),
            pl.BlockSpec((D_MODEL, D_MODEL), const),
            pl.BlockSpec((1, D_MODEL), const),
            pl.BlockSpec((D_MODEL, D_FF), const),
            pl.BlockSpec((D_MODEL, D_FF), const),
            pl.BlockSpec((D_FF, D_MODEL), const),
            pl.BlockSpec((1, D_MODEL), const),
        ],
        out_specs=pl.BlockSpec((tm, D_MODEL), lambda i: (i, 0)),
        out_shape=jax.ShapeDtypeStruct((n, D_MODEL), F32),
        compiler_params=pltpu.CompilerParams(dimension_semantics=("arbitrary",),
                                             vmem_limit_bytes=56 * 1024 * 1024),
        name="ffn",
    )(x, mix, w_out, norm_ffn, w_gate, w_up, w_down, norm_final)


def _mixer_kernel(layer, chunk, t_valid, *refs):
    C = chunk
    masked = t_valid < C
    it = iter(refs)
    p_ref = next(it)
    vf_in_ref = next(it) if layer > 0 else None
    gdn_s0, conv0, rwkv_s0, shift0, hgrn_s0 = (next(it) for _ in range(5))
    conv_w, alog, dtb, gdn_nw = (next(it) for _ in range(4))
    mu, w0, w2, a0, a2, g2 = (next(it) for _ in range(6))
    if layer > 0:
        v0, v1, v2 = (next(it) for _ in range(3))
    kk_w, ka_w, rk_w, lnw, lnb = (next(it) for _ in range(5))
    hlb, hgrn_nw = (next(it) for _ in range(2))
    mix_ref = next(it)
    vf_out_ref = next(it) if layer == 0 else None
    gdn_s, conv_s, rwkv_s, shift_s, hgrn_s = (next(it) for _ in range(5))
    xc_ref, pbx_ref = (next(it) for _ in range(2))

    t = pl.program_id(1)
    keep = CONV_W - 1
    c0 = CARRY_ROWS - keep
    s0 = CARRY_ROWS - 1

    @pl.when(t == 0)
    def _():
        gdn_s[...] = gdn_s0[...]
        rwkv_s[...] = rwkv_s0[...]
        hgrn_s[...] = hgrn_s0[...]
        xc_ref[c0:CARRY_ROWS, :] = conv0[...]
        pbx_ref[s0:CARRY_ROWS, :] = shift0[...]

    row = lax.broadcasted_iota(jnp.int32, (C, C), 0)
    col = lax.broadcasted_iota(jnp.int32, (C, C), 1)
    causal = col <= row
    strict = col < row
    eye = col == row
    tri = jnp.where(causal, 1.0, 0.0).astype(F32)
    if masked:
        valid = lax.broadcasted_iota(jnp.int32, (C, 1), 0) < t_valid

    def mask_rows(x):
        return jnp.where(valid, x, 0.0) if masked else x

    n_double = 0
    while (2 << n_double) < C:
        n_double += 1

    xc_ref[CARRY_ROWS:CARRY_ROWS + C, :] = p_ref[:, OFF_QKV:OFF_QKV + 3 * A_WIDTH]
    cw = conv_w[...]
    y = xc_ref[c0:c0 + C, :] * cw[0:1]
    for j in range(1, CONV_W):
        y = y + xc_ref[c0 + j:c0 + j + C, :] * cw[j:j + 1]
    new_buf = xc_ref[c0 + t_valid:c0 + t_valid + keep, :]
    conv_s[...] = new_buf
    xc_ref[c0:CARRY_ROWS, :] = new_buf
    qkv = _silu(y)
    z = p_ref[:, OFF_Z:OFF_Z + A_WIDTH]
    ba = p_ref[:, OFF_BA:OFF_BA + LANES]
    beta_all = mask_rows(_sigmoid(ba))
    g_all = mask_rows(-jnp.exp(alog[...]) * _softplus(ba + dtb[...]))
    gcum = _mm_sel(tri, g_all)
    gcum_t = gcum.T
    gnw = gdn_nw[...]
    for h in range(H_A):
        qh = _head(qkv, h)
        kh = _head(qkv, H_A + h)
        vh = mask_rows(_head(qkv, 2 * H_A + h))
        qh = qh * lax.rsqrt(jnp.sum(qh * qh, -1, keepdims=True) + 1e-6) * HEAD_DIM ** -0.5
        kh = mask_rows(kh * lax.rsqrt(jnp.sum(kh * kh, -1, keepdims=True) + 1e-6))
        beta = beta_all[:, GATE_LANE_BETA + h:GATE_LANE_BETA + h + 1]
        gc = gcum[:, GATE_LANE_A + h:GATE_LANE_A + h + 1]
        gr = gcum_t[GATE_LANE_A + h:GATE_LANE_A + h + 1, :]
        gl = gcum[C - 1:C, GATE_LANE_A + h:GATE_LANE_A + h + 1]
        decay = jnp.where(causal, jnp.exp(jnp.where(causal, gc - gr, 0.0)), 0.0)
        m = jnp.where(strict, beta * _mm_nt(kh, kh) * decay, 0.0)
        t_off = _unit_lower_inverse_offdiag(-m, C)
        egc = jnp.exp(gc)
        rhs = jnp.concatenate([vh * beta, kh * (beta * egc)], axis=-1)
        sol = rhs + _mm(t_off, rhs)
        u = sol[:, :HEAD_DIM]
        w = sol[:, HEAD_DIM:]
        qk = _mm_nt(qh, kh) * decay
        s = gdn_s[h]
        v_new = u - _mm(w, s)
        o = _mm(qh * egc, s) + _mm(qk, v_new)
        gdn_s[h] = s * jnp.exp(gl) + _mm_tn(kh * jnp.exp(gl - gc), v_new)
        o = o * lax.rsqrt(jnp.mean(o * o, -1, keepdims=True) + RMS_EPS) * gnw * _silu(_head(z, h))
        mix_ref[:, h * HEAD_DIM:(h + 1) * HEAD_DIM] = o.astype(mix_ref.dtype)

    pb = p_ref[:, OFF_B:OFF_B + B_COLS]
    pbx_ref[CARRY_ROWS:CARRY_ROWS + C, :] = pb
    prev = pbx_ref[s0:s0 + C, :]
    new_shift = pbx_ref[s0 + t_valid:s0 + t_valid + 1, :]
    shift_s[...] = new_shift
    pbx_ref[s0:CARRY_ROWS, :] = new_shift
    xs = pb + (prev - pb) * mu[...]
    r_all = xs[:, 0:B_WIDTH]
    k_all = xs[:, B_WIDTH:2 * B_WIDTH]
    v_all = xs[:, 2 * B_WIDTH:3 * B_WIDTH]
    o_w = 3 * B_WIDTH
    wd = xs[:, o_w:o_w + W_LORA]
    ad = xs[:, o_w + W_LORA:o_w + W_LORA + A_LORA]
    gd = xs[:, o_w + W_LORA + A_LORA:o_w + W_LORA + A_LORA + G_LORA]
    w_log = -_softplus(-(w0[...] + _mm(jnp.tanh(wd), w2[...]))) - 0.5
    log_decay = mask_rows(-jnp.exp(w_log))
    a_all = _sigmoid(a0[...] + _mm(ad, a2[...]))
    g_out = _mm(_sigmoid(gd), g2[...])
    if layer == 0:
        vf_out_ref[...] = v_all
    else:
        mixv = _sigmoid(v0[...] + _mm(_mm(v_all, v1[...]), v2[...]))
        v_all = v_all + (vf_in_ref[...] - v_all) * mixv
    v_all = mask_rows(v_all)
    kkx = k_all * kk_w[...]
    k_all = mask_rows(k_all * (1.0 + (a_all - 1.0) * ka_w[...]))
    cum = _mm_sel(tri, log_decay)
    rk = rk_w[...]
    ln_w = lnw[...]
    ln_b = lnb[...]
    for h in range(H_B):
        lw = _head(log_decay, h)
        c = _head(cum, h)
        kk = _head(kkx, h)
        kk = mask_rows(kk * lax.rsqrt(jnp.sum(kk * kk, -1, keepdims=True) + 1e-6))
        ah = _head(a_all, h)
        kh = _head(k_all, h)
        vh = _head(v_all, h)
        rh = _head(r_all, h)
        bh = kk * ah
        cl = c[C - 1:C, :]
        einv = jnp.exp(-c)
        ehat = jnp.exp(cl - c)
        a_t = -kk * jnp.exp(c - lw)
        r_t = rh * jnp.exp(c)
        ar = jnp.concatenate([a_t, r_t], axis=0)
        gb = _mm_nt(ar, bh * einv)
        gk = _mm_nt(ar, kh * einv)
        l_ab = jnp.where(strict, gb[:C], 0.0)
        l_ak = jnp.where(strict, gk[:C], 0.0)
        m_rb = jnp.where(causal, gb[C:], 0.0)
        m_rk = jnp.where(causal, gk[C:], 0.0)
        t_off = _unit_lower_inverse_offdiag(l_ab, C)
        s = rwkv_s[h]
        ars = _mm_nt(ar, s)
        rhs_u = ars[:C] + _mm(l_ak, vh)
        u = rhs_u + _mm(t_off, rhs_u)
        yh = ars[C:] + _mm(m_rb, u) + _mm(m_rk, vh)
        uv = jnp.concatenate([u, vh], axis=0)
        bk = jnp.concatenate([bh * ehat, kh * ehat], axis=0)
        rwkv_s[h] = s * jnp.exp(cl) + _mm_tn(uv, bk)
        mean = jnp.mean(yh, -1, keepdims=True)
        yc = yh - mean
        yn = yc * lax.rsqrt(jnp.mean(yc * yc, -1, keepdims=True) + RWKV_LN_EPS)
        yn = yn * _head(ln_w, h) + _head(ln_b, h)
        bonus = jnp.sum(rh * kh * _head(rk, h), -1, keepdims=True) * vh
        out = (yn + bonus) * _head(g_out, h)
        lo = A_WIDTH + h * HEAD_DIM
        mix_ref[:, lo:lo + HEAD_DIM] = out.astype(mix_ref.dtype)

    pc = p_ref[:, OFF_C:OFF_C + C_COLS]
    q_all = _silu(pc[:, 0:C_WIDTH]) * HEAD_DIM ** -0.5
    f_all = pc[:, C_WIDTH:2 * C_WIDTH]
    i_all = pc[:, 2 * C_WIDTH:3 * C_WIDTH]
    go_all = pc[:, 3 * C_WIDTH:4 * C_WIDTH]
    hl = hlb[...]
    hmax = jnp.max(hl, axis=0, keepdims=True)
    he = jnp.exp(hl - hmax)
    sm = he / jnp.sum(he, axis=0, keepdims=True)
    lb_acc = sm[0:1]
    for mrow in range(1, layer + 1):
        lb_acc = lb_acc + sm[mrow:mrow + 1]
    lb = lb_acc - sm[0:1]
    k_all = mask_rows((1.0 - lb) * _sigmoid(-f_all))
    log_f = jnp.log1p(-k_all)
    bcum = _mm_sel(tri, log_f)
    lsel = []
    pair_masks = []
    levels = []
    ls = 0
    while (1 << ls) < C:
        levels.append(ls)
        ls += 1
    for ls in levels:
        blk = row >> ls
        start = blk << ls
        odd = (blk & 1) == 1
        sel_q = odd & (col >= start) & (col <= row)
        sel_k = (~odd) & (col > row) & (col <= start + ((1 << ls) - 1))
        lsel.append(jnp.where(sel_q | sel_k, 1.0, 0.0).astype(F32))
        pair_masks.append(odd & ((col >> ls) == blk - 1))
    e_lvls = jnp.exp(_mm_sel(jnp.concatenate(lsel, axis=0), log_f))
    eb = jnp.exp(bcum)
    bl = bcum[C - 1:C, :]
    k_hat = k_all * jnp.exp(bl - bcum)
    e_col = jnp.exp(_colsum_bcast(log_f))
    hnw = hgrn_nw[...]
    for h in range(H_C):
        qh = _head(q_all, h)
        kh = _head(k_all, h)
        vh = _head(i_all, h)
        att = jnp.where(eye, jnp.sum(qh * kh, -1, keepdims=True), 0.0)
        for n, ls in enumerate(levels):
            e = _head(e_lvls[n * C:(n + 1) * C], h)
            att = att + jnp.where(pair_masks[n], _mm_nt(qh * e, kh * e), 0.0)
        s = hgrn_s[h]
        o = _mm(qh * _head(eb, h), s) + _mm(att, vh)
        hgrn_s[h] = e_col[h * HEAD_DIM:(h + 1) * HEAD_DIM, 0:HEAD_DIM] * s + _mm_tn(_head(k_hat, h), vh)
        o = (o * lax.rsqrt(jnp.mean(o * o, -1, keepdims=True) + RMS_EPS) * hnw
             * _silu(_head(go_all, h)))
        lo = A_WIDTH + B_WIDTH + h * HEAD_DIM
        mix_ref[:, lo:lo + HEAD_DIM] = o.astype(mix_ref.dtype)


def _mixers(layer, p, vfirst, states, lp, chunk, t_valid):
    bsz, tp, _ = p.shape
    nt = tp // chunk
    assert nt * chunk == tp and (nt == 1 or t_valid == chunk)
    gdn0, conv0, rwkv0, shift0, hgrn0 = states

    def per_bt(width):
        return pl.BlockSpec((None, chunk, width), lambda b, t: (b, t, 0))

    def per_b(shape):
        nd = len(shape)
        return pl.BlockSpec((None,) + shape, lambda b, t: (b,) + (0,) * nd)

    def const(arr):
        nd = arr.ndim
        return pl.BlockSpec(arr.shape, lambda b, t: (0,) * nd)

    state_shapes = [(H_A, HEAD_DIM, HEAD_DIM), (CONV_W - 1, 3 * A_WIDTH), (H_B, HEAD_DIM, HEAD_DIM),
                    (1, B_COLS), (H_C, HEAD_DIM, HEAD_DIM)]
    inputs = [p]
    in_specs = [per_bt(P_COLS)]
    if layer > 0:
        inputs.append(vfirst)
        in_specs.append(per_bt(B_WIDTH))
    inputs += [gdn0, conv0, rwkv0, shift0, hgrn0]
    in_specs += [per_b(s) for s in state_shapes]
    names = ["conv_w", "alog", "dtb", "gdn_nw", "mu", "w0", "w2", "a0", "a2", "g2"]
    if layer > 0:
        names += ["v0", "v1", "v2"]
    names += ["kk_w", "ka_w", "rk_w", "lnw", "lnb", "hlb", "hgrn_nw"]
    for nm in names:
        inputs.append(lp[nm])
        in_specs.append(const(lp[nm]))

    out_shape = [jax.ShapeDtypeStruct((bsz, tp, D_MODEL), BF)]
    out_specs = [per_bt(D_MODEL)]
    if layer == 0:
        out_shape.append(jax.ShapeDtypeStruct((bsz, tp, B_WIDTH), F32))
        out_specs.append(per_bt(B_WIDTH))
    for s in state_shapes:
        out_shape.append(jax.ShapeDtypeStruct((bsz,) + s, F32))
        out_specs.append(per_b(s))

    outs = pl.pallas_call(
        functools.partial(_mixer_kernel, layer, chunk, t_valid),
        grid=(bsz, nt),
        in_specs=in_specs,
        out_specs=out_specs,
        out_shape=out_shape,
        scratch_shapes=[pltpu.VMEM((CARRY_ROWS + chunk, 3 * A_WIDTH), F32),
                        pltpu.VMEM((CARRY_ROWS + chunk, B_COLS), F32)],
        compiler_params=pltpu.CompilerParams(dimension_semantics=("arbitrary", "arbitrary")),
        name=f"mixers_l{layer}_c{chunk}",
    )(*inputs)
    mix = outs[0]
    if layer == 0:
        vfirst = outs[1]
        new_states = outs[2:]
    else:
        new_states = outs[1:]
    return mix, vfirst, new_states


def _trunk(x, states, W):
    bsz, t, _ = x.shape
    if t % PROMPT_CHUNK == 0:
        chunk, tp, t_valid = PROMPT_CHUNK, t, PROMPT_CHUNK
    else:
        assert t <= DECODE_CHUNK
        chunk, tp, t_valid = DECODE_CHUNK, DECODE_CHUNK, t
    n = bsz * t
    xf = x.reshape(n, D_MODEL)
    st_gdn, st_conv, st_rwkv, st_shift, st_hgrn = states
    new = ([], [], [], [], [])
    vfirst = None
    for l in range(DEPTH):
        lp = W["layers"][l]
        p = _inproj(xf, lp["norm_mix"], lp["w_in"]).reshape(bsz, t, P_COLS)
        if tp != t:
            p = jnp.pad(p, ((0, 0), (0, tp - t), (0, 0)))
        layer_states = (st_gdn[l], st_conv[l], st_rwkv[l], st_shift[l][:, None, :], st_hgrn[l])
        mix, vfirst, outs = _mixers(l, p, vfirst, layer_states, lp, chunk, t_valid)
        mix = mix[:, :t].reshape(n, D_MODEL)
        xf = _ffn(xf, mix, lp["w_out"], lp["norm_ffn"], lp["w_gate"], lp["w_up"], lp["w_down"],
                  W["norm_final"], final=(l == DEPTH - 1))
        for lst, s in zip(new, outs):
            lst.append(s)
    y = xf.reshape(bsz, t, D_MODEL)
    gdn, conv, rwkv, shift, hgrn = (jnp.stack(lst) for lst in new)
    return y, (gdn, conv, rwkv, shift[:, :, 0, :], hgrn)


def _prepare_weights(norm_mix, w_in, w_out, gdn_conv_w, gdn_a_log, gdn_dt_bias, gdn_norm_w,
                     rwkv_mu, rwkv_w0, rwkv_w2, rwkv_a0, rwkv_a2, rwkv_g2, rwkv_v0, rwkv_v1, rwkv_v2,
                     rwkv_k_k, rwkv_k_a, rwkv_r_k, rwkv_ln_w, rwkv_ln_b,
                     hgrn_lower_bounds, hgrn_norm_w, norm_ffn, w_gate, w_up, w_down, norm_final):
    depth = w_in.shape[0]
    ba_cols = w_in[:, :, 4 * A_WIDTH:A_COLS]
    pad = jnp.zeros((depth, D_MODEL, LANES - 2 * H_A), w_in.dtype)
    w_in_r = jnp.concatenate([w_in[:, :, :4 * A_WIDTH], w_in[:, :, A_COLS:], ba_cols, pad], axis=-1).astype(BF)

    def gate_row(v):
        z_lo = jnp.zeros((depth, GATE_LANE_A), F32)
        z_hi = jnp.zeros((depth, LANES - GATE_LANE_A - H_A), F32)
        return jnp.concatenate([z_lo, v.astype(F32), z_hi], axis=-1)[:, None, :]

    alog = gate_row(gdn_a_log)
    dtb = gate_row(gdn_dt_bias)
    layers = []
    for l in range(depth):
        lp = {
            "norm_mix": norm_mix[l][None], "w_in": w_in_r[l], "w_out": w_out[l].astype(BF),
            "norm_ffn": norm_ffn[l][None], "w_gate": w_gate[l].astype(BF), "w_up": w_up[l].astype(BF),
            "w_down": w_down[l].astype(BF),
            "conv_w": gdn_conv_w[l], "alog": alog[l], "dtb": dtb[l], "gdn_nw": gdn_norm_w[l][None],
            "mu": rwkv_mu[l][None], "w0": rwkv_w0[l][None], "w2": rwkv_w2[l], "a0": rwkv_a0[l][None],
            "a2": rwkv_a2[l], "g2": rwkv_g2[l],
            "kk_w": rwkv_k_k[l][None], "ka_w": rwkv_k_a[l][None], "rk_w": rwkv_r_k[l].reshape(1, B_WIDTH),
            "lnw": rwkv_ln_w[l][None], "lnb": rwkv_ln_b[l][None],
            "hlb": hgrn_lower_bounds, "hgrn_nw": hgrn_norm_w[l][None],
        }
        if l > 0:
            lp.update(v0=rwkv_v0[l - 1][None], v1=rwkv_v1[l - 1], v2=rwkv_v2[l - 1])
        layers.append(lp)
    return {"layers": layers, "norm_final": norm_final[None]}


def kernel(x_prompt, x_sample, state_gdn, state_gdn_conv, state_rwkv, state_rwkv_shift, state_hgrn, norm_mix, w_in, w_out, gdn_conv_w, gdn_a_log, gdn_dt_bias, gdn_norm_w, rwkv_mu, rwkv_w0, rwkv_w2, rwkv_a0, rwkv_a2, rwkv_g2, rwkv_v0, rwkv_v1, rwkv_v2, rwkv_k_k, rwkv_k_a, rwkv_r_k, rwkv_ln_w, rwkv_ln_b, hgrn_lower_bounds, hgrn_norm_w, norm_ffn, w_gate, w_up, w_down, norm_final):
    W = _prepare_weights(norm_mix, w_in, w_out, gdn_conv_w, gdn_a_log, gdn_dt_bias, gdn_norm_w,
                         rwkv_mu, rwkv_w0, rwkv_w2, rwkv_a0, rwkv_a2, rwkv_g2, rwkv_v0, rwkv_v1, rwkv_v2,
                         rwkv_k_k, rwkv_k_a, rwkv_r_k, rwkv_ln_w, rwkv_ln_b,
                         hgrn_lower_bounds, hgrn_norm_w, norm_ffn, w_gate, w_up, w_down, norm_final)
    b = x_prompt.shape[0]
    depth = w_in.shape[0]
    init_prompt = (
        jnp.zeros((depth, b, H_A, HEAD_DIM, HEAD_DIM), F32),
        jnp.zeros((depth, b, CONV_W - 1, 3 * A_WIDTH), F32),
        jnp.zeros((depth, b, H_B, HEAD_DIM, HEAD_DIM), F32),
        jnp.zeros((depth, b, B_COLS), F32),
        jnp.zeros((depth, b, H_C, HEAD_DIM, HEAD_DIM), F32),
    )
    y_prompt, (gdn_p, conv_p, rwkv_p, shift_p, hgrn_p) = _trunk(x_prompt, init_prompt, W)
    y_sample, (gdn_s, conv_s, rwkv_s, shift_s, hgrn_s) = _trunk(
        x_sample, (state_gdn, state_gdn_conv, state_rwkv, state_rwkv_shift, state_hgrn), W)
    return (y_prompt, y_sample, gdn_p, gdn_s, conv_p, conv_s, rwkv_p, rwkv_s, shift_p, shift_s, hgrn_p, hgrn_s)
```

```python
import functools

import jax
import jax.numpy as jnp
from jax import lax
from jax.experimental import pallas as pl
from jax.experimental.pallas import tpu as pltpu

D_MODEL = 1024
DEPTH = 4
HEAD_DIM = 64
A_WIDTH = 384
B_WIDTH = 384
C_WIDTH = 256
H_A = A_WIDTH // HEAD_DIM
H_B = B_WIDTH // HEAD_DIM
H_C = C_WIDTH // HEAD_DIM
CONV_W = 4
W_LORA = 64
A_LORA = 64
G_LORA = 128
V_LORA = 32
A_COLS = 4 * A_WIDTH + 2 * H_A
B_COLS = 3 * B_WIDTH + W_LORA + A_LORA + G_LORA
C_COLS = 4 * C_WIDTH
D_FF = 2816
RMS_EPS = 1e-6
RWKV_LN_EPS = 64e-5

LANES = 128
OFF_QKV = 0
OFF_Z = 3 * A_WIDTH
OFF_B = 4 * A_WIDTH
OFF_C = OFF_B + B_COLS
OFF_BA = OFF_C + C_COLS
P_COLS = OFF_BA + LANES
GATE_LANE_BETA = 0
GATE_LANE_A = H_A

PROMPT_CHUNK = 64
DECODE_CHUNK = 8
CARRY_ROWS = 8

BF = jnp.bfloat16
F32 = jnp.float32


def _mm(a, b):
    return jnp.dot(a.astype(BF), b.astype(BF), preferred_element_type=F32)


def _mm_nt(a, b):
    return lax.dot_general(a.astype(BF), b.astype(BF), (((1,), (1,)), ((), ())),
                           preferred_element_type=F32)


def _mm_tn(a, b):
    return lax.dot_general(a.astype(BF), b.astype(BF), (((0,), (0,)), ((), ())),
                           preferred_element_type=F32)


def _split3(x):
    x1 = x.astype(BF)
    r1 = x - x1.astype(F32)
    x2 = r1.astype(BF)
    x3 = (r1 - x2.astype(F32)).astype(BF)
    return x1, x2, x3


def _mm_sel(sel, x):
    s = sel.astype(BF)
    x1, x2, x3 = _split3(x)
    return (jnp.dot(s, x1, preferred_element_type=F32) + jnp.dot(s, x2, preferred_element_type=F32)
            + jnp.dot(s, x3, preferred_element_type=F32))


def _colsum_bcast(x):
    ones = jnp.ones((x.shape[0], LANES), BF)
    x1, x2, x3 = _split3(x)
    dn = (((0,), (0,)), ((), ()))
    return (lax.dot_general(x1, ones, dn, preferred_element_type=F32)
            + lax.dot_general(x2, ones, dn, preferred_element_type=F32)
            + lax.dot_general(x3, ones, dn, preferred_element_type=F32))


def _sigmoid(x):
    return 1.0 / (1.0 + jnp.exp(-x))


def _silu(x):
    return x * _sigmoid(x)


def _softplus(x):
    return jnp.maximum(x, 0.0) + jnp.log1p(jnp.exp(-jnp.abs(x)))


def _rmsnorm(x, w):
    return x * lax.rsqrt(jnp.mean(x * x, -1, keepdims=True) + RMS_EPS) * w


def _lockstep(gens):
    gens = list(gens)
    while gens:
        alive = []
        for g in gens:
            try:
                next(g)
                alive.append(g)
            except StopIteration:
                pass
        gens = alive
        if gens:
            yield


def _head(x, h):
    return x[:, h * HEAD_DIM:(h + 1) * HEAD_DIM]


def _inproj_kernel(x_ref, nw_ref, w_ref, o_ref):
    h = _rmsnorm(x_ref[...], nw_ref[...]).astype(BF)
    step = 512
    for c in range(0, P_COLS, step):
        o_ref[:, c:c + step] = jnp.dot(h, w_ref[:, c:c + step], preferred_element_type=F32)


def _inproj(x, norm_w, w):
    n = x.shape[0]
    tm = min(512, n)
    assert n % tm == 0
    return pl.pallas_call(
        _inproj_kernel,
        grid=(n // tm,),
        in_specs=[
            pl.BlockSpec((tm, D_MODEL), lambda i: (i, 0)),
            pl.BlockSpec((1, D_MODEL), lambda i: (0, 0)),
            pl.BlockSpec((D_MODEL, P_COLS), lambda i: (0, 0)),
        ],
        out_specs=pl.BlockSpec((tm, P_COLS), lambda i: (i, 0)),
        out_shape=jax.ShapeDtypeStruct((n, P_COLS), F32),
        compiler_params=pltpu.CompilerParams(dimension_semantics=("arbitrary",)),
        name="inproj",
    )(x, norm_w, w)


def _ffn_kernel(final, x_ref, mix_ref, wo_ref, nf_ref, wg_ref, wu_ref, wd_ref, nfin_ref, o_ref):
    x = x_ref[...] + jnp.dot(mix_ref[...], wo_ref[...], preferred_element_type=F32)
    h = _rmsnorm(x, nf_ref[...]).astype(BF)
    half = D_FF // 2
    acc = x
    for c in range(0, D_FF, half):
        gate = jnp.dot(h, wg_ref[:, c:c + half], preferred_element_type=F32)
        up = jnp.dot(h, wu_ref[:, c:c + half], preferred_element_type=F32)
        u = (_silu(gate) * up).astype(BF)
        acc = acc + jnp.dot(u, wd_ref[c:c + half, :], preferred_element_type=F32)
    if final:
        acc = _rmsnorm(acc, nfin_ref[...])
    o_ref[...] = acc


def _ffn(x, mix, w_out, norm_ffn, w_gate, w_up, w_down, norm_final, final):
    n = x.shape[0]
    tm = min(256, n)
    assert n % tm == 0
    const = lambda i: (0, 0)
    return pl.pallas_call(
        functools.partial(_ffn_kernel, final),
        grid=(n // tm,),
        in_specs=[
            pl.BlockSpec((tm, D_MODEL), lambda i: (i, 0)),
            pl.BlockSpec((tm, D_MODEL), lambda i: (i, 0)),
            pl.BlockSpec((D_MODEL, D_MODEL), const),
            pl.BlockSpec((1, D_MODEL), const),
            pl.BlockSpec((D_MODEL, D_FF), const),
            pl.BlockSpec((D_MODEL, D_FF), const),
            pl.BlockSpec((D_FF, D_MODEL), const),
            pl.BlockSpec((1, D_MODEL), const),
        ],
        out_specs=pl.BlockSpec((tm, D_MODEL), lambda i: (i, 0)),
        out_shape=jax.ShapeDtypeStruct((n, D_MODEL), F32),
        compiler_params=pltpu.CompilerParams(dimension_semantics=("arbitrary",),
                                             vmem_limit_bytes=56 * 1024 * 1024),
        name="ffn",
    )(x, mix, w_out, norm_ffn, w_gate, w_up, w_down, norm_final)


def _mixer_kernel(layer, chunk, t_valid, *refs):
    C = chunk
    masked = t_valid < C
    it = iter(refs)
    p_ref = next(it)
    vf_in_ref = next(it) if layer > 0 else None
    gdn_s0, conv0, rwkv_s0, shift0, hgrn_s0 = (next(it) for _ in range(5))
    conv_w, alog, dtb, gdn_nw = (next(it) for _ in range(4))
    mu, w0, w2, a0, a2, g2 = (next(it) for _ in range(6))
    if layer > 0:
        v0, v1, v2 = (next(it) for _ in range(3))
    kk_w, ka_w, rk_w, lnw, lnb = (next(it) for _ in range(5))
    hlb, hgrn_nw = (next(it) for _ in range(2))
    mix_ref = next(it)
    vf_out_ref = next(it) if layer == 0 else None
    gdn_s, conv_s, rwkv_s, shift_s, hgrn_s = (next(it) for _ in range(5))
    xc_ref, pbx_ref = (next(it) for _ in range(2))

    t = pl.program_id(1)
    keep = CONV_W - 1
    c0 = CARRY_ROWS - keep
    s0 = CARRY_ROWS - 1

    @pl.when(t == 0)
    def _():
        gdn_s[...] = gdn_s0[...]
        rwkv_s[...] = rwkv_s0[...]
        hgrn_s[...] = hgrn_s0[...]
        xc_ref[c0:CARRY_ROWS, :] = conv0[...]
        pbx_ref[s0:CARRY_ROWS, :] = shift0[...]

    row = lax.broadcasted_iota(jnp.int32, (C, C), 0)
    col = lax.broadcasted_iota(jnp.int32, (C, C), 1)
    causal = col <= row
    strict = col < row
    eye = col == row
    tri = jnp.where(causal, 1.0, 0.0).astype(F32)
    if masked:
        valid = lax.broadcasted_iota(jnp.int32, (C, 1), 0) < t_valid

    def mask_rows(x):
        return jnp.where(valid, x, 0.0) if masked else x

    def run_solve(p, r):
        span = 1
        while True:
            r_hi = r.astype(BF)
            r_lo = (r - r_hi.astype(F32)).astype(BF)
            p_bf = p.astype(BF)
            pr = (jnp.dot(p_bf, r_hi, preferred_element_type=F32)
                  + jnp.dot(p_bf, r_lo, preferred_element_type=F32))
            span *= 2
            p_next = _mm(p, p) if span < C else None
            yield
            r = r + pr
            if span >= C:
                return r
            p = p_next


    def gdn_mixer():
        xc_ref[CARRY_ROWS:CARRY_ROWS + C, :] = p_ref[:, OFF_QKV:OFF_QKV + 3 * A_WIDTH]
        cw = conv_w[...]
        y = xc_ref[c0:c0 + C, :] * cw[0:1]
        for j in range(1, CONV_W):
            y = y + xc_ref[c0 + j:c0 + j + C, :] * cw[j:j + 1]
        new_buf = xc_ref[c0 + t_valid:c0 + t_valid + keep, :]
        conv_s[...] = new_buf
        xc_ref[c0:CARRY_ROWS, :] = new_buf
        qkv = _silu(y)
        z = p_ref[:, OFF_Z:OFF_Z + A_WIDTH]
        ba = p_ref[:, OFF_BA:OFF_BA + LANES]
        beta_all = mask_rows(_sigmoid(ba))
        g_all = mask_rows(-jnp.exp(alog[...]) * _softplus(ba + dtb[...]))
        gcum = _mm_sel(tri, g_all)
        yield
        gcum_t = gcum.T
        gnw = gdn_nw[...]

        def head(h):
            qh = _head(qkv, h)
            kh = _head(qkv, H_A + h)
            vh = mask_rows(_head(qkv, 2 * H_A + h))
            qh = qh * lax.rsqrt(jnp.sum(qh * qh, -1, keepdims=True) + 1e-6) * HEAD_DIM ** -0.5
            kh = mask_rows(kh * lax.rsqrt(jnp.sum(kh * kh, -1, keepdims=True) + 1e-6))
            beta = beta_all[:, GATE_LANE_BETA + h:GATE_LANE_BETA + h + 1]
            gc = gcum[:, GATE_LANE_A + h:GATE_LANE_A + h + 1]
            gr = gcum_t[GATE_LANE_A + h:GATE_LANE_A + h + 1, :]
            gl = gcum[C - 1:C, GATE_LANE_A + h:GATE_LANE_A + h + 1]
            decay = jnp.where(causal, jnp.exp(jnp.where(causal, gc - gr, 0.0)), 0.0)
            egc = jnp.exp(gc)
            kk = _mm_nt(kh, kh)
            qk = _mm_nt(qh, kh)
            qs = _mm(qh * egc, gdn_s[h])
            yield
            neg_m = jnp.where(strict, -(beta * kk * decay), 0.0)
            rhs = jnp.concatenate([vh * beta, kh * (beta * egc)], axis=-1)
            sol = yield from run_solve(neg_m, rhs)
            u = sol[:, :HEAD_DIM]
            w = sol[:, HEAD_DIM:]
            ws = _mm(w, gdn_s[h])
            yield
            v_new = u - ws
            o2 = _mm(qk * decay, v_new)
            ds = _mm_tn(kh * jnp.exp(gl - gc), v_new)
            yield
            o = qs + o2
            gdn_s[h] = gdn_s[h] * jnp.exp(gl) + ds
            o = o * lax.rsqrt(jnp.mean(o * o, -1, keepdims=True) + RMS_EPS) * gnw * _silu(_head(z, h))
            mix_ref[:, h * HEAD_DIM:(h + 1) * HEAD_DIM] = o.astype(mix_ref.dtype)

        yield from _lockstep([head(h) for h in range(H_A)])

    def rwkv_mixer():
        pb = p_ref[:, OFF_B:OFF_B + B_COLS]
        pbx_ref[CARRY_ROWS:CARRY_ROWS + C, :] = pb
        prev = pbx_ref[s0:s0 + C, :]
        new_shift = pbx_ref[s0 + t_valid:s0 + t_valid + 1, :]
        shift_s[...] = new_shift
        pbx_ref[s0:CARRY_ROWS, :] = new_shift
        xs = pb + (prev - pb) * mu[...]
        r_all = xs[:, 0:B_WIDTH]
        k_all = xs[:, B_WIDTH:2 * B_WIDTH]
        v_all = xs[:, 2 * B_WIDTH:3 * B_WIDTH]
        o_w = 3 * B_WIDTH
        wd = xs[:, o_w:o_w + W_LORA]
        ad = xs[:, o_w + W_LORA:o_w + W_LORA + A_LORA]
        gd = xs[:, o_w + W_LORA + A_LORA:o_w + W_LORA + A_LORA + G_LORA]
        w_lora = _mm(jnp.tanh(wd), w2[...])
        a_lora = _mm(ad, a2[...])
        g_out = _mm(_sigmoid(gd), g2[...])
        if layer == 0:
            vf_out_ref[...] = v_all
        else:
            v_lora = _mm(v_all, v1[...])
        yield
        w_log = -_softplus(-(w0[...] + w_lora)) - 0.5
        log_decay = mask_rows(-jnp.exp(w_log))
        cum = _mm_sel(tri, log_decay)
        if layer > 0:
            v_mix = _mm(v_lora, v2[...])
        yield
        a_all = _sigmoid(a0[...] + a_lora)
        if layer > 0:
            mixv = _sigmoid(v0[...] + v_mix)
            v_all = v_all + (vf_in_ref[...] - v_all) * mixv
        v_all = mask_rows(v_all)
        kkx = k_all * kk_w[...]
        k_all = mask_rows(k_all * (1.0 + (a_all - 1.0) * ka_w[...]))
        rk = rk_w[...]
        ln_w = lnw[...]
        ln_b = lnb[...]

        def head(h):
            lw = _head(log_decay, h)
            c = _head(cum, h)
            kk = _head(kkx, h)
            kk = mask_rows(kk * lax.rsqrt(jnp.sum(kk * kk, -1, keepdims=True) + 1e-6))
            ah = _head(a_all, h)
            kh = _head(k_all, h)
            vh = _head(v_all, h)
            rh = _head(r_all, h)
            bh = kk * ah
            cl = c[C - 1:C, :]
            einv = jnp.exp(-c)
            ehat = jnp.exp(cl - c)
            a_t = -kk * jnp.exp(c - lw)
            r_t = rh * jnp.exp(c)
            ar = jnp.concatenate([a_t, r_t], axis=0)
            gb = _mm_nt(ar, bh * einv)
            gk = _mm_nt(ar, kh * einv)
            ars = _mm_nt(ar, rwkv_s[h])
            yield
            l_ab = jnp.where(strict, gb[:C], 0.0)
            l_ak = jnp.where(strict, gk[:C], 0.0)
            m_rb = jnp.where(causal, gb[C:], 0.0)
            m_rk = jnp.where(causal, gk[C:], 0.0)
            lakv = _mm(l_ak, vh)
            mrkv = _mm(m_rk, vh)
            yield
            u = yield from run_solve(l_ab, ars[:C] + lakv)
            uv = jnp.concatenate([u, vh], axis=0)
            bk = jnp.concatenate([bh * ehat, kh * ehat], axis=0)
            mrbu = _mm(m_rb, u)
            ds = _mm_tn(uv, bk)
            yield
            yh = ars[C:] + mrbu + mrkv
            rwkv_s[h] = rwkv_s[h] * jnp.exp(cl) + ds
            mean = jnp.mean(yh, -1, keepdims=True)
            yc = yh - mean
            yn = yc * lax.rsqrt(jnp.mean(yc * yc, -1, keepdims=True) + RWKV_LN_EPS)
            yn = yn * _head(ln_w, h) + _head(ln_b, h)
            bonus = jnp.sum(rh * kh * _head(rk, h), -1, keepdims=True) * vh
            out = (yn + bonus) * _head(g_out, h)
            lo = A_WIDTH + h * HEAD_DIM
            mix_ref[:, lo:lo + HEAD_DIM] = out.astype(mix_ref.dtype)

        yield from _lockstep([head(h) for h in range(H_B)])

    def hgrn_mixer():
        pc = p_ref[:, OFF_C:OFF_C + C_COLS]
        q_all = _silu(pc[:, 0:C_WIDTH]) * HEAD_DIM ** -0.5
        f_all = pc[:, C_WIDTH:2 * C_WIDTH]
        i_all = pc[:, 2 * C_WIDTH:3 * C_WIDTH]
        go_all = pc[:, 3 * C_WIDTH:4 * C_WIDTH]
        hl = hlb[...]
        hmax = jnp.max(hl, axis=0, keepdims=True)
        he = jnp.exp(hl - hmax)
        sm = he / jnp.sum(he, axis=0, keepdims=True)
        lb_acc = sm[0:1]
        for mrow in range(1, layer + 1):
            lb_acc = lb_acc + sm[mrow:mrow + 1]
        lb = lb_acc - sm[0:1]
        k_all = mask_rows((1.0 - lb) * _sigmoid(-f_all))
        log_f = jnp.log1p(-k_all)
        lsel = []
        pair_masks = []
        levels = []
        ls = 0
        while (1 << ls) < C:
            levels.append(ls)
            ls += 1
        for ls in levels:
            blk = row >> ls
            start = blk << ls
            odd = (blk & 1) == 1
            sel_q = odd & (col >= start) & (col <= row)
            sel_k = (~odd) & (col > row) & (col <= start + ((1 << ls) - 1))
            lsel.append(jnp.where(sel_q | sel_k, 1.0, 0.0).astype(F32))
            pair_masks.append(odd & ((col >> ls) == blk - 1))
        bcum = _mm_sel(tri, log_f)
        lvl_sums = _mm_sel(jnp.concatenate(lsel, axis=0), log_f)
        col_sums = _colsum_bcast(log_f)
        yield
        e_lvls = jnp.exp(lvl_sums)
        eb = jnp.exp(bcum)
        bl = bcum[C - 1:C, :]
        k_hat = k_all * jnp.exp(bl - bcum)
        e_col = jnp.exp(col_sums)
        hnw = hgrn_nw[...]

        def head(h):
            qh = _head(q_all, h)
            kh = _head(k_all, h)
            vh = _head(i_all, h)
            parts = []
            for n in range(len(levels)):
                e = _head(e_lvls[n * C:(n + 1) * C], h)
                parts.append(_mm_nt(qh * e, kh * e))
            qs = _mm(qh * _head(eb, h), hgrn_s[h])
            ds = _mm_tn(_head(k_hat, h), vh)
            yield
            att = jnp.where(eye, jnp.sum(qh * kh, -1, keepdims=True), 0.0)
            for n in range(len(levels)):
                att = att + jnp.where(pair_masks[n], parts[n], 0.0)
            o2 = _mm(att, vh)
            yield
            o = qs + o2
            hgrn_s[h] = e_col[h * HEAD_DIM:(h + 1) * HEAD_DIM, 0:HEAD_DIM] * hgrn_s[h] + ds
            o = (o * lax.rsqrt(jnp.mean(o * o, -1, keepdims=True) + RMS_EPS) * hnw
                 * _silu(_head(go_all, h)))
            lo = A_WIDTH + B_WIDTH + h * HEAD_DIM
            mix_ref[:, lo:lo + HEAD_DIM] = o.astype(mix_ref.dtype)

        yield from _lockstep([head(h) for h in range(H_C)])

    for _ in _lockstep([gdn_mixer(), rwkv_mixer(), hgrn_mixer()]):
        pass


def _mixers(layer, p, vfirst, states, lp, chunk, t_valid):
    bsz, tp, _ = p.shape
    nt = tp // chunk
    assert nt * chunk == tp and (nt == 1 or t_valid == chunk)
    gdn0, conv0, rwkv0, shift0, hgrn0 = states

    def per_bt(width):
        return pl.BlockSpec((None, chunk, width), lambda b, t: (b, t, 0))

    def per_b(shape):
        nd = len(shape)
        return pl.BlockSpec((None,) + shape, lambda b, t: (b,) + (0,) * nd)

    def const(arr):
        nd = arr.ndim
        return pl.BlockSpec(arr.shape, lambda b, t: (0,) * nd)

    state_shapes = [(H_A, HEAD_DIM, HEAD_DIM), (CONV_W - 1, 3 * A_WIDTH), (H_B, HEAD_DIM, HEAD_DIM),
                    (1, B_COLS), (H_C, HEAD_DIM, HEAD_DIM)]
    inputs = [p]
    in_specs = [per_bt(P_COLS)]
    if layer > 0:
        inputs.append(vfirst)
        in_specs.append(per_bt(B_WIDTH))
    inputs += [gdn0, conv0, rwkv0, shift0, hgrn0]
    in_specs += [per_b(s) for s in state_shapes]
    names = ["conv_w", "alog", "dtb", "gdn_nw", "mu", "w0", "w2", "a0", "a2", "g2"]
    if layer > 0:
        names += ["v0", "v1", "v2"]
    names += ["kk_w", "ka_w", "rk_w", "lnw", "lnb", "hlb", "hgrn_nw"]
    for nm in names:
        inputs.append(lp[nm])
        in_specs.append(const(lp[nm]))

    out_shape = [jax.ShapeDtypeStruct((bsz, tp, D_MODEL), BF)]
    out_specs = [per_bt(D_MODEL)]
    if layer == 0:
        out_shape.append(jax.ShapeDtypeStruct((bsz, tp, B_WIDTH), F32))
        out_specs.append(per_bt(B_WIDTH))
    for s in state_shapes:
        out_shape.append(jax.ShapeDtypeStruct((bsz,) + s, F32))
        out_specs.append(per_b(s))

    outs = pl.pallas_call(
        functools.partial(_mixer_kernel, layer, chunk, t_valid),
        grid=(bsz, nt),
        in_specs=in_specs,
        out_specs=out_specs,
        out_shape=out_shape,
        scratch_shapes=[pltpu.VMEM((CARRY_ROWS + chunk, 3 * A_WIDTH), F32),
                        pltpu.VMEM((CARRY_ROWS + chunk, B_COLS), F32)],
        compiler_params=pltpu.CompilerParams(dimension_semantics=("arbitrary", "arbitrary")),
        name=f"mixers_l{layer}_c{chunk}",
    )(*inputs)
    mix = outs[0]
    if layer == 0:
        vfirst = outs[1]
        new_states = outs[2:]
    else:
        new_states = outs[1:]
    return mix, vfirst, new_states


def _trunk(x, states, W):
    bsz, t, _ = x.shape
    if t % PROMPT_CHUNK == 0:
        chunk, tp, t_valid = PROMPT_CHUNK, t, PROMPT_CHUNK
    else:
        assert t <= DECODE_CHUNK
        chunk, tp, t_valid = DECODE_CHUNK, DECODE_CHUNK, t
    n = bsz * t
    xf = x.reshape(n, D_MODEL)
    st_gdn, st_conv, st_rwkv, st_shift, st_hgrn = states
    new = ([], [], [], [], [])
    vfirst = None
    for l in range(DEPTH):
        lp = W["layers"][l]
        p = _inproj(xf, lp["norm_mix"], lp["w_in"]).reshape(bsz, t, P_COLS)
        if tp != t:
            p = jnp.pad(p, ((0, 0), (0, tp - t), (0, 0)))
        layer_states = (st_gdn[l], st_conv[l], st_rwkv[l], st_shift[l][:, None, :], st_hgrn[l])
        mix, vfirst, outs = _mixers(l, p, vfirst, layer_states, lp, chunk, t_valid)
        mix = mix[:, :t].reshape(n, D_MODEL)
        xf = _ffn(xf, mix, lp["w_out"], lp["norm_ffn"], lp["w_gate"], lp["w_up"], lp["w_down"],
                  W["norm_final"], final=(l == DEPTH - 1))
        for lst, s in zip(new, outs):
            lst.append(s)
    y = xf.reshape(bsz, t, D_MODEL)
    gdn, conv, rwkv, shift, hgrn = (jnp.stack(lst) for lst in new)
    return y, (gdn, conv, rwkv, shift[:, :, 0, :], hgrn)


def _prepare_weights(norm_mix, w_in, w_out, gdn_conv_w, gdn_a_log, gdn_dt_bias, gdn_norm_w,
                     rwkv_mu, rwkv_w0, rwkv_w2, rwkv_a0, rwkv_a2, rwkv_g2, rwkv_v0, rwkv_v1, rwkv_v2,
                     rwkv_k_k, rwkv_k_a, rwkv_r_k, rwkv_ln_w, rwkv_ln_b,
                     hgrn_lower_bounds, hgrn_norm_w, norm_ffn, w_gate, w_up, w_down, norm_final):
    depth = w_in.shape[0]
    ba_cols = w_in[:, :, 4 * A_WIDTH:A_COLS]
    pad = jnp.zeros((depth, D_MODEL, LANES - 2 * H_A), w_in.dtype)
    w_in_r = jnp.concatenate([w_in[:, :, :4 * A_WIDTH], w_in[:, :, A_COLS:], ba_cols, pad], axis=-1).astype(BF)

    def gate_row(v):
        z_lo = jnp.zeros((depth, GATE_LANE_A), F32)
        z_hi = jnp.zeros((depth, LANES - GATE_LANE_A - H_A), F32)
        return jnp.concatenate([z_lo, v.astype(F32), z_hi], axis=-1)[:, None, :]

    alog = gate_row(gdn_a_log)
    dtb = gate_row(gdn_dt_bias)
    layers = []
    for l in range(depth):
        lp = {
            "norm_mix": norm_mix[l][None], "w_in": w_in_r[l], "w_out": w_out[l].astype(BF),
            "norm_ffn": norm_ffn[l][None], "w_gate": w_gate[l].astype(BF), "w_up": w_up[l].astype(BF),
            "w_down": w_down[l].astype(BF),
            "conv_w": gdn_conv_w[l], "alog": alog[l], "dtb": dtb[l], "gdn_nw": gdn_norm_w[l][None],
            "mu": rwkv_mu[l][None], "w0": rwkv_w0[l][None], "w2": rwkv_w2[l], "a0": rwkv_a0[l][None],
            "a2": rwkv_a2[l], "g2": rwkv_g2[l],
            "kk_w": rwkv_k_k[l][None], "ka_w": rwkv_k_a[l][None], "rk_w": rwkv_r_k[l].reshape(1, B_WIDTH),
            "lnw": rwkv_ln_w[l][None], "lnb": rwkv_ln_b[l][None],
            "hlb": hgrn_lower_bounds, "hgrn_nw": hgrn_norm_w[l][None],
        }
        if l > 0:
            lp.update(v0=rwkv_v0[l - 1][None], v1=rwkv_v1[l - 1], v2=rwkv_v2[l - 1])
        layers.append(lp)
    return {"layers": layers, "norm_final": norm_final[None]}


def kernel(x_prompt, x_sample, state_gdn, state_gdn_conv, state_rwkv, state_rwkv_shift, state_hgrn, norm_mix, w_in, w_out, gdn_conv_w, gdn_a_log, gdn_dt_bias, gdn_norm_w, rwkv_mu, rwkv_w0, rwkv_w2, rwkv_a0, rwkv_a2, rwkv_g2, rwkv_v0, rwkv_v1, rwkv_v2, rwkv_k_k, rwkv_k_a, rwkv_r_k, rwkv_ln_w, rwkv_ln_b, hgrn_lower_bounds, hgrn_norm_w, norm_ffn, w_gate, w_up, w_down, norm_final):
    W = _prepare_weights(norm_mix, w_in, w_out, gdn_conv_w, gdn_a_log, gdn_dt_bias, gdn_norm_w,
                         rwkv_mu, rwkv_w0, rwkv_w2, rwkv_a0, rwkv_a2, rwkv_g2, rwkv_v0, rwkv_v1, rwkv_v2,
                         rwkv_k_k, rwkv_k_a, rwkv_r_k, rwkv_ln_w, rwkv_ln_b,
                         hgrn_lower_bounds, hgrn_norm_w, norm_ffn, w_gate, w_up, w_down, norm_final)
    b = x_prompt.shape[0]
    depth = w_in.shape[0]
    init_prompt = (
        jnp.zeros((depth, b, H_A, HEAD_DIM, HEAD_DIM), F32),
        jnp.zeros((depth, b, CONV_W - 1, 3 * A_WIDTH), F32),
        jnp.zeros((depth, b, H_B, HEAD_DIM, HEAD_DIM), F32),
        jnp.zeros((depth, b, B_COLS), F32),
        jnp.zeros((depth, b, H_C, HEAD_DIM, HEAD_DIM), F32),
    )
    y_prompt, (gdn_p, conv_p, rwkv_p, shift_p, hgrn_p) = _trunk(x_prompt, init_prompt, W)
    y_sample, (gdn_s, conv_s, rwkv_s, shift_s, hgrn_s) = _trunk(
        x_sample, (state_gdn, state_gdn_conv, state_rwkv, state_rwkv_shift, state_hgrn), W)
    return (y_prompt, y_sample, gdn_p, gdn_s, conv_p, conv_s, rwkv_p, rwkv_s, shift_p, shift_s, hgrn_p, hgrn_s)
```

```python
import functools

import jax
import jax.numpy as jnp
from jax import lax
from jax.experimental import pallas as pl
from jax.experimental.pallas import tpu as pltpu

D_MODEL = 1024
DEPTH = 4
HEAD_DIM = 64
A_WIDTH = 384
B_WIDTH = 384
C_WIDTH = 256
H_A = A_WIDTH // HEAD_DIM
H_B = B_WIDTH // HEAD_DIM
H_C = C_WIDTH // HEAD_DIM
CONV_W = 4
W_LORA = 64
A_LORA = 64
G_LORA = 128
V_LORA = 32
A_COLS = 4 * A_WIDTH + 2 * H_A
B_COLS = 3 * B_WIDTH + W_LORA + A_LORA + G_LORA
C_COLS = 4 * C_WIDTH
D_FF = 2816
RMS_EPS = 1e-6
RWKV_LN_EPS = 64e-5

LANES = 128
OFF_QKV = 0
OFF_Z = 3 * A_WIDTH
OFF_B = 4 * A_WIDTH
OFF_C = OFF_B + B_COLS
OFF_BA = OFF_C + C_COLS
P_COLS = OFF_BA + LANES
GATE_LANE_BETA = 0
GATE_LANE_A = H_A

PROMPT_CHUNK = 64
DECODE_ROWS = 8
DECODE_GROUP = 1
CARRY_ROWS = 8

STATE_SHAPES = ((H_A, HEAD_DIM, HEAD_DIM), (CONV_W - 1, 3 * A_WIDTH), (H_B, HEAD_DIM, HEAD_DIM),
                (1, B_COLS), (H_C, HEAD_DIM, HEAD_DIM))

BF = jnp.bfloat16
F32 = jnp.float32


def _mm(a, b):
    return jnp.dot(a.astype(BF), b.astype(BF), preferred_element_type=F32)


def _mm_nt(a, b):
    return lax.dot_general(a.astype(BF), b.astype(BF), (((1,), (1,)), ((), ())),
                           preferred_element_type=F32)


def _mm_tn(a, b):
    return lax.dot_general(a.astype(BF), b.astype(BF), (((0,), (0,)), ((), ())),
                           preferred_element_type=F32)


def _split3(x):
    x1 = x.astype(BF)
    r1 = x - x1.astype(F32)
    x2 = r1.astype(BF)
    x3 = (r1 - x2.astype(F32)).astype(BF)
    return x1, x2, x3


def _mm_sel(sel, x):
    s = sel.astype(BF)
    x1, x2, x3 = _split3(x)
    return (jnp.dot(s, x1, preferred_element_type=F32) + jnp.dot(s, x2, preferred_element_type=F32)
            + jnp.dot(s, x3, preferred_element_type=F32))


def _mm_sel_r(x, sel):
    s = sel.astype(BF)
    x1, x2, x3 = _split3(x)
    return (jnp.dot(x1, s, preferred_element_type=F32) + jnp.dot(x2, s, preferred_element_type=F32)
            + jnp.dot(x3, s, preferred_element_type=F32))


def _colsum_bcast(x):
    ones = jnp.ones((x.shape[0], LANES), BF)
    x1, x2, x3 = _split3(x)
    dn = (((0,), (0,)), ((), ()))
    return (lax.dot_general(x1, ones, dn, preferred_element_type=F32)
            + lax.dot_general(x2, ones, dn, preferred_element_type=F32)
            + lax.dot_general(x3, ones, dn, preferred_element_type=F32))


def _sigmoid(x):
    return 1.0 / (1.0 + jnp.exp(-x))


def _silu(x):
    return x * _sigmoid(x)


def _softplus(x):
    return jnp.maximum(x, 0.0) + jnp.log1p(jnp.exp(-jnp.abs(x)))


def _rmsnorm(x, w):
    return x * lax.rsqrt(jnp.mean(x * x, -1, keepdims=True) + RMS_EPS) * w


def _lockstep(gens):
    gens = list(gens)
    while gens:
        alive = []
        for g in gens:
            try:
                next(g)
                alive.append(g)
            except StopIteration:
                pass
        gens = alive
        if gens:
            yield


def _head(x, h):
    return x[:, h * HEAD_DIM:(h + 1) * HEAD_DIM]


def _inproj_kernel(x_ref, nw_ref, w_ref, o_ref):
    h = _rmsnorm(x_ref[...], nw_ref[...]).astype(BF)
    step = 512
    for c in range(0, P_COLS, step):
        o_ref[:, c:c + step] = jnp.dot(h, w_ref[:, c:c + step], preferred_element_type=F32)


def _inproj(x, norm_w, w):
    n = x.shape[0]
    tm = min(512, n)
    assert n % tm == 0
    return pl.pallas_call(
        _inproj_kernel,
        grid=(n // tm,),
        in_specs=[
            pl.BlockSpec((tm, D_MODEL), lambda i: (i, 0)),
            pl.BlockSpec((1, D_MODEL), lambda i: (0, 0)),
            pl.BlockSpec((D_MODEL, P_COLS), lambda i: (0, 0)),
        ],
        out_specs=pl.BlockSpec((tm, P_COLS), lambda i: (i, 0)),
        out_shape=jax.ShapeDtypeStruct((n, P_COLS), F32),
        compiler_params=pltpu.CompilerParams(dimension_semantics=("arbitrary",)),
        name="inproj",
    )(x, norm_w, w)


def _ffn_kernel(final, x_ref, mix_ref, wo_ref, nf_ref, wg_ref, wu_ref, wd_ref, nfin_ref, o_ref):
    x = x_ref[...] + jnp.dot(mix_ref[...], wo_ref[...], preferred_element_type=F32)
    h = _rmsnorm(x, nf_ref[...]).astype(BF)
    half = D_FF // 2
    acc = x
    for c in range(0, D_FF, half):
        gate = jnp.dot(h, wg_ref[:, c:c + half], preferred_element_type=F32)
        up = jnp.dot(h, wu_ref[:, c:c + half], preferred_element_type=F32)
        u = (_silu(gate) * up).astype(BF)
        acc = acc + jnp.dot(u, wd_ref[c:c + half, :], preferred_element_type=F32)
    if final:
        acc = _rmsnorm(acc, nfin_ref[...])
    o_ref[...] = acc


def _ffn(x, mix, w_out, norm_ffn, w_gate, w_up, w_down, norm_final, final):
    n = x.shape[0]
    tm = min(256, n)
    assert n % tm == 0
    const = lambda i: (0, 0)
    return pl.pallas_call(
        functools.partial(_ffn_kernel, final),
        grid=(n // tm,),
        in_specs=[
            pl.BlockSpec((tm, D_MODEL), lambda i: (i, 0)),
            pl.BlockSpec((tm, D_MODEL), lambda i: (i, 0)),
            pl.BlockSpec((D_MODEL, D_MODEL), const),
            pl.BlockSpec((1, D_MODEL), const),
            pl.BlockSpec((D_MODEL, D_FF), const),
            pl.BlockSpec((D_MODEL, D_FF), const),
            pl.BlockSpec((D_FF, D_MODEL), const),
            pl.BlockSpec((1, D_MODEL), const),
        ],
        out_specs=pl.BlockSpec((tm, D_MODEL), lambda i: (i, 0)),
        out_shape=jax.ShapeDtypeStruct((n, D_MODEL), F32),
        compiler_params=pltpu.CompilerParams(dimension_semantics=("arbitrary",),
                                             vmem_limit_bytes=56 * 1024 * 1024),
        name="ffn",
    )(x, mix, w_out, norm_ffn, w_gate, w_up, w_down, norm_final)


def _mixer_kernel(layer, rows, single, group, *refs):
    C = rows
    it = iter(refs)
    p_ref = next(it)
    vf_in_ref = next(it) if layer > 0 else None
    gdn_s0, conv0, rwkv_s0, shift0, hgrn_s0 = (next(it) for _ in range(5))
    conv_w, alog, dtb, gdn_nw = (next(it) for _ in range(4))
    mu, w0, w2, a0, a2, g2 = (next(it) for _ in range(6))
    if layer > 0:
        v0, v1, v2 = (next(it) for _ in range(3))
    kk_w, ka_w, rk_w, lnw, lnb = (next(it) for _ in range(5))
    hlb, hgrn_nw = (next(it) for _ in range(2))
    for _ in STATE_SHAPES:
        next(it)
    mix_ref = next(it)
    vf_out_ref = next(it) if layer == 0 else None
    gdn_s, conv_s, rwkv_s, shift_s, hgrn_s = (next(it) for _ in range(5))
    xc_ref, pbx_ref = (next(it) for _ in range(2))

    t = pl.program_id(1)
    keep = CONV_W - 1
    c0 = CARRY_ROWS - keep
    s0 = CARRY_ROWS - 1
    t_valid = 1 if single else C

    if single:
        gdn_old, rwkv_old, hgrn_old = gdn_s0, rwkv_s0, hgrn_s0
        xc_ref[:, c0:CARRY_ROWS, :] = conv0[...]
        pbx_ref[:, s0:CARRY_ROWS, :] = shift0[...]
    else:
        gdn_old, rwkv_old, hgrn_old = gdn_s, rwkv_s, hgrn_s

        @pl.when(t == 0)
        def _():
            gdn_s[...] = gdn_s0[...]
            rwkv_s[...] = rwkv_s0[...]
            hgrn_s[...] = hgrn_s0[...]
            xc_ref[:, c0:CARRY_ROWS, :] = conv0[...]
            pbx_ref[:, s0:CARRY_ROWS, :] = shift0[...]

    row = lax.broadcasted_iota(jnp.int32, (C, C), 0)
    col = lax.broadcasted_iota(jnp.int32, (C, C), 1)
    causal = col <= row
    strict = col < row
    eye = col == row
    tri = jnp.where(causal, 1.0, 0.0).astype(F32)
    if single:
        valid = lax.broadcasted_iota(jnp.int32, (C, 1), 0) < 1

    def mask_rows(x):
        return jnp.where(valid, x, 0.0) if single else x

    def load_rows(ref, g, lo, width):
        x = ref[g, :, lo:lo + width]
        return jnp.broadcast_to(x, (C, width)) if single else x

    def store_rows(ref, g, lo, val):
        val = val[0:1] if single else val
        ref[g, :, lo:lo + val.shape[-1]] = val.astype(ref.dtype)

    def cumsum_rows(x):
        if single:
            return jnp.broadcast_to(x[0:1], x.shape)
        return _mm_sel(tri, x)

    def run_solve(p, r):
        span = 1
        while True:
            r_hi = r.astype(BF)
            r_lo = (r - r_hi.astype(F32)).astype(BF)
            p_bf = p.astype(BF)
            pr = (jnp.dot(p_bf, r_hi, preferred_element_type=F32)
                  + jnp.dot(p_bf, r_lo, preferred_element_type=F32))
            span *= 2
            p_next = _mm(p, p) if span < C else None
            yield
            r = r + pr
            if span >= C:
                return r
            p = p_next


    def gdn_mixer(g):
        xc_ref[g, CARRY_ROWS:CARRY_ROWS + C, :] = load_rows(p_ref, g, OFF_QKV, 3 * A_WIDTH)
        cw = conv_w[...]
        y = xc_ref[g, c0:c0 + C, :] * cw[0:1]
        for j in range(1, CONV_W):
            y = y + xc_ref[g, c0 + j:c0 + j + C, :] * cw[j:j + 1]
        new_buf = xc_ref[g, c0 + t_valid:c0 + t_valid + keep, :]
        conv_s[g] = new_buf
        xc_ref[g, c0:CARRY_ROWS, :] = new_buf
        qkv = _silu(y)
        z = load_rows(p_ref, g, OFF_Z, A_WIDTH)
        ba = load_rows(p_ref, g, OFF_BA, LANES)
        beta_all = mask_rows(_sigmoid(ba))
        g_all = mask_rows(-jnp.exp(alog[...]) * _softplus(ba + dtb[...]))
        gcum = cumsum_rows(g_all)
        yield
        if not single:
            gcum_t = gcum.T
        src_lane = lax.broadcasted_iota(jnp.int32, (LANES, A_WIDTH), 0)
        dst_head = lax.broadcasted_iota(jnp.int32, (LANES, A_WIDTH), 1) // HEAD_DIM
        beta_x = _mm_sel_r(beta_all, jnp.where(src_lane == dst_head + GATE_LANE_BETA, 1.0, 0.0))
        gcum_x = _mm_sel_r(gcum, jnp.where(src_lane == dst_head + GATE_LANE_A, 1.0, 0.0))
        yield
        gnw = gdn_nw[...]

        def head(h):
            qh = _head(qkv, h)
            kh = _head(qkv, H_A + h)
            vh = mask_rows(_head(qkv, 2 * H_A + h))
            qh = qh * lax.rsqrt(jnp.sum(qh * qh, -1, keepdims=True) + 1e-6) * HEAD_DIM ** -0.5
            kh = mask_rows(kh * lax.rsqrt(jnp.sum(kh * kh, -1, keepdims=True) + 1e-6))
            beta = _head(beta_x, h)
            gc = _head(gcum_x, h)
            gl = gc[C - 1:C, :]
            if single:
                decay = tri
            else:
                gr = gcum_t[GATE_LANE_A + h:GATE_LANE_A + h + 1, :]
                decay = jnp.where(causal, jnp.exp(jnp.where(causal, gc[:, :C] - gr, 0.0)), 0.0)
            egc = jnp.exp(gc)
            if not single:
                kk = _mm_nt(kh, kh)
            qk = _mm_nt(qh, kh)
            qs = _mm(qh * egc, gdn_old[g, h])
            yield
            if single:
                u = vh * beta
                w = kh * (beta * egc)
            else:
                rhs = jnp.concatenate([vh * beta, kh * (beta * egc)], axis=-1)
                neg_m = jnp.where(strict, -(beta[:, :C] * kk * decay), 0.0)
                sol = yield from run_solve(neg_m, rhs)
                u = sol[:, :HEAD_DIM]
                w = sol[:, HEAD_DIM:]
            ws = _mm(w, gdn_old[g, h])
            yield
            v_new = u - ws
            o2 = _mm(qk * decay, v_new)
            ds = _mm_tn(kh * jnp.exp(gl - gc), v_new)
            yield
            o = qs + o2
            gdn_s[g, h] = gdn_old[g, h] * jnp.exp(gl) + ds
            o = o * lax.rsqrt(jnp.mean(o * o, -1, keepdims=True) + RMS_EPS) * gnw * _silu(_head(z, h))
            store_rows(mix_ref, g, h * HEAD_DIM, o)

        yield from _lockstep([head(h) for h in range(H_A)])

    def rwkv_mixer(g):
        pb = load_rows(p_ref, g, OFF_B, B_COLS)
        pbx_ref[g, CARRY_ROWS:CARRY_ROWS + C, :] = pb
        prev = pbx_ref[g, s0:s0 + C, :]
        new_shift = pbx_ref[g, s0 + t_valid:s0 + t_valid + 1, :]
        shift_s[g] = new_shift
        pbx_ref[g, s0:CARRY_ROWS, :] = new_shift
        xs = pb + (prev - pb) * mu[...]
        r_all = xs[:, 0:B_WIDTH]
        k_all = xs[:, B_WIDTH:2 * B_WIDTH]
        v_all = xs[:, 2 * B_WIDTH:3 * B_WIDTH]
        o_w = 3 * B_WIDTH
        wd = xs[:, o_w:o_w + W_LORA]
        ad = xs[:, o_w + W_LORA:o_w + W_LORA + A_LORA]
        gd = xs[:, o_w + W_LORA + A_LORA:o_w + W_LORA + A_LORA + G_LORA]
        w_lora = _mm(jnp.tanh(wd), w2[...])
        a_lora = _mm(ad, a2[...])
        g_out = _mm(_sigmoid(gd), g2[...])
        if layer == 0:
            store_rows(vf_out_ref, g, 0, v_all)
        else:
            v_lora = _mm(v_all, v1[...])
        yield
        w_log = -_softplus(-(w0[...] + w_lora)) - 0.5
        log_decay = mask_rows(-jnp.exp(w_log))
        cum = cumsum_rows(log_decay)
        if layer > 0:
            v_mix = _mm(v_lora, v2[...])
        yield
        a_all = _sigmoid(a0[...] + a_lora)
        if layer > 0:
            mixv = _sigmoid(v0[...] + v_mix)
            v_all = v_all + (load_rows(vf_in_ref, g, 0, B_WIDTH) - v_all) * mixv
        v_all = mask_rows(v_all)
        kkx = k_all * kk_w[...]
        k_all = mask_rows(k_all * (1.0 + (a_all - 1.0) * ka_w[...]))
        rk = rk_w[...]
        ln_w = lnw[...]
        ln_b = lnb[...]

        def head(h):
            lw = _head(log_decay, h)
            c = _head(cum, h)
            kk = _head(kkx, h)
            kk = mask_rows(kk * lax.rsqrt(jnp.sum(kk * kk, -1, keepdims=True) + 1e-6))
            ah = _head(a_all, h)
            kh = _head(k_all, h)
            vh = _head(v_all, h)
            rh = _head(r_all, h)
            bh = kk * ah
            cl = c[C - 1:C, :]
            einv = jnp.exp(-c)
            ehat = jnp.exp(cl - c)
            a_t = -kk * jnp.exp(c - lw)
            r_t = rh * jnp.exp(c)
            ar = jnp.concatenate([a_t, r_t], axis=0)
            gb = _mm_nt(ar, bh * einv)
            gk = _mm_nt(ar, kh * einv)
            ars = _mm_nt(ar, rwkv_old[g, h])
            yield
            m_rb = jnp.where(causal, gb[C:], 0.0)
            m_rk = jnp.where(causal, gk[C:], 0.0)
            mrkv = _mm(m_rk, vh)
            if single:
                u = ars[:C]
            else:
                l_ab = jnp.where(strict, gb[:C], 0.0)
                l_ak = jnp.where(strict, gk[:C], 0.0)
                lakv = _mm(l_ak, vh)
                yield
                u = yield from run_solve(l_ab, ars[:C] + lakv)
            uv = jnp.concatenate([u, vh], axis=0)
            bk = jnp.concatenate([bh * ehat, kh * ehat], axis=0)
            mrbu = _mm(m_rb, u)
            ds = _mm_tn(uv, bk)
            yield
            yh = ars[C:] + mrbu + mrkv
            rwkv_s[g, h] = rwkv_old[g, h] * jnp.exp(cl) + ds
            mean = jnp.mean(yh, -1, keepdims=True)
            yc = yh - mean
            yn = yc * lax.rsqrt(jnp.mean(yc * yc, -1, keepdims=True) + RWKV_LN_EPS)
            yn = yn * _head(ln_w, h) + _head(ln_b, h)
            bonus = jnp.sum(rh * kh * _head(rk, h), -1, keepdims=True) * vh
            out = (yn + bonus) * _head(g_out, h)
            store_rows(mix_ref, g, A_WIDTH + h * HEAD_DIM, out)

        yield from _lockstep([head(h) for h in range(H_B)])

    def hgrn_mixer(g):
        pc = load_rows(p_ref, g, OFF_C, C_COLS)
        q_all = _silu(pc[:, 0:C_WIDTH]) * HEAD_DIM ** -0.5
        f_all = pc[:, C_WIDTH:2 * C_WIDTH]
        i_all = pc[:, 2 * C_WIDTH:3 * C_WIDTH]
        go_all = pc[:, 3 * C_WIDTH:4 * C_WIDTH]
        hl = hlb[...]
        hmax = jnp.max(hl, axis=0, keepdims=True)
        he = jnp.exp(hl - hmax)
        sm = he / jnp.sum(he, axis=0, keepdims=True)
        lb_acc = sm[0:1]
        for mrow in range(1, layer + 1):
            lb_acc = lb_acc + sm[mrow:mrow + 1]
        lb = lb_acc - sm[0:1]
        k_all = mask_rows((1.0 - lb) * _sigmoid(-f_all))
        log_f = jnp.log1p(-k_all)
        lsel = []
        pair_masks = []
        levels = []
        if not single:
            ls = 0
            while (1 << ls) < C:
                levels.append(ls)
                ls += 1
        for ls in levels:
            blk = row >> ls
            start = blk << ls
            odd = (blk & 1) == 1
            sel_q = odd & (col >= start) & (col <= row)
            sel_k = (~odd) & (col > row) & (col <= start + ((1 << ls) - 1))
            lsel.append(jnp.where(sel_q | sel_k, 1.0, 0.0).astype(F32))
            pair_masks.append(odd & ((col >> ls) == blk - 1))
        bcum = cumsum_rows(log_f)
        if levels:
            lvl_sums = _mm_sel(jnp.concatenate(lsel, axis=0), log_f)
        col_sums = _colsum_bcast(log_f)
        yield
        if levels:
            e_lvls = jnp.exp(lvl_sums)
        eb = jnp.exp(bcum)
        bl = bcum[C - 1:C, :]
        k_hat = k_all * jnp.exp(bl - bcum)
        e_col = jnp.exp(col_sums)
        hnw = hgrn_nw[...]

        def head(h):
            qh = _head(q_all, h)
            kh = _head(k_all, h)
            vh = _head(i_all, h)
            parts = []
            for n in range(len(levels)):
                e = _head(e_lvls[n * C:(n + 1) * C], h)
                parts.append(_mm_nt(qh * e, kh * e))
            qs = _mm(qh * _head(eb, h), hgrn_old[g, h])
            ds = _mm_tn(_head(k_hat, h), vh)
            yield
            att = jnp.where(eye, jnp.sum(qh * kh, -1, keepdims=True), 0.0)
            for n in range(len(levels)):
                att = att + jnp.where(pair_masks[n], parts[n], 0.0)
            o2 = _mm(att, vh)
            yield
            o = qs + o2
            hgrn_s[g, h] = (e_col[h * HEAD_DIM:(h + 1) * HEAD_DIM, 0:HEAD_DIM] * hgrn_old[g, h]
                            + ds)
            o = (o * lax.rsqrt(jnp.mean(o * o, -1, keepdims=True) + RMS_EPS) * hnw
                 * _silu(_head(go_all, h)))
            store_rows(mix_ref, g, A_WIDTH + B_WIDTH + h * HEAD_DIM, o)

        yield from _lockstep([head(h) for h in range(H_C)])

    mixers = []
    for g in range(group):
        mixers += [gdn_mixer(g), rwkv_mixer(g), hgrn_mixer(g)]
    for _ in _lockstep(mixers):
        pass


def _mixers(layer, p, vfirst, states_in, state_layer, acc, lp, rows, single, group):
    bsz, t, _ = p.shape
    t_blk = 1 if single else rows
    nt = t // t_blk
    assert nt * t_blk == t and bsz % group == 0 and (not single or t == 1)

    def per_bt(width):
        return pl.BlockSpec((group, t_blk, width), lambda b, t_: (b, t_, 0))

    def per_b(shape, lyr):
        nd = len(shape)
        return pl.BlockSpec((None, group) + shape, lambda b, t_: (lyr, b) + (0,) * nd)

    def const(arr):
        nd = arr.ndim
        return pl.BlockSpec(arr.shape, lambda b, t_: (0,) * nd)

    inputs = [p]
    in_specs = [per_bt(P_COLS)]
    if layer > 0:
        inputs.append(vfirst)
        in_specs.append(per_bt(B_WIDTH))
    inputs += list(states_in)
    in_specs += [per_b(s, state_layer) for s in STATE_SHAPES]
    names = ["conv_w", "alog", "dtb", "gdn_nw", "mu", "w0", "w2", "a0", "a2", "g2"]
    if layer > 0:
        names += ["v0", "v1", "v2"]
    names += ["kk_w", "ka_w", "rk_w", "lnw", "lnb", "hlb", "hgrn_nw"]
    for nm in names:
        inputs.append(lp[nm])
        in_specs.append(const(lp[nm]))
    acc_first = len(inputs)
    inputs += list(acc)
    in_specs += [pl.BlockSpec(memory_space=pl.ANY) for _ in acc]

    out_shape = [jax.ShapeDtypeStruct((bsz, t, D_MODEL), BF)]
    out_specs = [per_bt(D_MODEL)]
    if layer == 0:
        out_shape.append(jax.ShapeDtypeStruct((bsz, t, B_WIDTH), F32))
        out_specs.append(per_bt(B_WIDTH))
    state_first = len(out_shape)
    for s, a in zip(STATE_SHAPES, acc):
        out_shape.append(jax.ShapeDtypeStruct(a.shape, F32))
        out_specs.append(per_b(s, layer))
    aliases = {acc_first + i: state_first + i for i in range(len(acc))}

    outs = pl.pallas_call(
        functools.partial(_mixer_kernel, layer, rows, single, group),
        grid=(bsz // group, nt),
        in_specs=in_specs,
        out_specs=out_specs,
        out_shape=out_shape,
        input_output_aliases=aliases,
        scratch_shapes=[pltpu.VMEM((group, CARRY_ROWS + rows, 3 * A_WIDTH), F32),
                        pltpu.VMEM((group, CARRY_ROWS + rows, B_COLS), F32)],
        compiler_params=pltpu.CompilerParams(dimension_semantics=("arbitrary", "arbitrary")),
        name=f"mixers_l{layer}_r{rows}",
    )(*inputs)
    mix = outs[0]
    if layer == 0:
        vfirst = outs[1]
    return mix, vfirst, tuple(outs[state_first:])


def _trunk(x, states_in, layered_states, W):
    bsz, t, _ = x.shape
    if t == 1:
        rows, single, group = DECODE_ROWS, True, DECODE_GROUP
    else:
        assert t % PROMPT_CHUNK == 0
        rows, single, group = PROMPT_CHUNK, False, 1
    n = bsz * t
    xf = x.reshape(n, D_MODEL)
    acc = tuple(jnp.zeros((DEPTH, bsz) + s, F32) for s in STATE_SHAPES)
    vfirst = None
    for l in range(DEPTH):
        lp = W["layers"][l]
        p = _inproj(xf, lp["norm_mix"], lp["w_in"]).reshape(bsz, t, P_COLS)
        mix, vfirst, acc = _mixers(l, p, vfirst, states_in, l if layered_states else 0, acc, lp,
                                   rows, single, group)
        xf = _ffn(xf, mix.reshape(n, D_MODEL), lp["w_out"], lp["norm_ffn"], lp["w_gate"], lp["w_up"],
                  lp["w_down"], W["norm_final"], final=(l == DEPTH - 1))
    y = xf.reshape(bsz, t, D_MODEL)
    gdn, conv, rwkv, shift, hgrn = acc
    return y, (gdn, conv, rwkv, shift[:, :, 0, :], hgrn)


def _prepare_weights(norm_mix, w_in, w_out, gdn_conv_w, gdn_a_log, gdn_dt_bias, gdn_norm_w,
                     rwkv_mu, rwkv_w0, rwkv_w2, rwkv_a0, rwkv_a2, rwkv_g2, rwkv_v0, rwkv_v1, rwkv_v2,
                     rwkv_k_k, rwkv_k_a, rwkv_r_k, rwkv_ln_w, rwkv_ln_b,
                     hgrn_lower_bounds, hgrn_norm_w, norm_ffn, w_gate, w_up, w_down, norm_final):
    depth = w_in.shape[0]
    ba_cols = w_in[:, :, 4 * A_WIDTH:A_COLS]
    pad = jnp.zeros((depth, D_MODEL, LANES - 2 * H_A), w_in.dtype)
    w_in_r = jnp.concatenate([w_in[:, :, :4 * A_WIDTH], w_in[:, :, A_COLS:], ba_cols, pad], axis=-1).astype(BF)

    def gate_row(v):
        z_lo = jnp.zeros((depth, GATE_LANE_A), F32)
        z_hi = jnp.zeros((depth, LANES - GATE_LANE_A - H_A), F32)
        return jnp.concatenate([z_lo, v.astype(F32), z_hi], axis=-1)[:, None, :]

    alog = gate_row(gdn_a_log)
    dtb = gate_row(gdn_dt_bias)
    layers = []
    for l in range(depth):
        lp = {
            "norm_mix": norm_mix[l][None], "w_in": w_in_r[l], "w_out": w_out[l].astype(BF),
            "norm_ffn": norm_ffn[l][None], "w_gate": w_gate[l].astype(BF), "w_up": w_up[l].astype(BF),
            "w_down": w_down[l].astype(BF),
            "conv_w": gdn_conv_w[l], "alog": alog[l], "dtb": dtb[l], "gdn_nw": gdn_norm_w[l][None],
            "mu": rwkv_mu[l][None], "w0": rwkv_w0[l][None], "w2": rwkv_w2[l], "a0": rwkv_a0[l][None],
            "a2": rwkv_a2[l], "g2": rwkv_g2[l],
            "kk_w": rwkv_k_k[l][None], "ka_w": rwkv_k_a[l][None], "rk_w": rwkv_r_k[l].reshape(1, B_WIDTH),
            "lnw": rwkv_ln_w[l][None], "lnb": rwkv_ln_b[l][None],
            "hlb": hgrn_lower_bounds, "hgrn_nw": hgrn_norm_w[l][None],
        }
        if l > 0:
            lp.update(v0=rwkv_v0[l - 1][None], v1=rwkv_v1[l - 1], v2=rwkv_v2[l - 1])
        layers.append(lp)
    return {"layers": layers, "norm_final": norm_final[None]}


def kernel(x_prompt, x_sample, state_gdn, state_gdn_conv, state_rwkv, state_rwkv_shift, state_hgrn, norm_mix, w_in, w_out, gdn_conv_w, gdn_a_log, gdn_dt_bias, gdn_norm_w, rwkv_mu, rwkv_w0, rwkv_w2, rwkv_a0, rwkv_a2, rwkv_g2, rwkv_v0, rwkv_v1, rwkv_v2, rwkv_k_k, rwkv_k_a, rwkv_r_k, rwkv_ln_w, rwkv_ln_b, hgrn_lower_bounds, hgrn_norm_w, norm_ffn, w_gate, w_up, w_down, norm_final):
    W = _prepare_weights(norm_mix, w_in, w_out, gdn_conv_w, gdn_a_log, gdn_dt_bias, gdn_norm_w,
                         rwkv_mu, rwkv_w0, rwkv_w2, rwkv_a0, rwkv_a2, rwkv_g2, rwkv_v0, rwkv_v1, rwkv_v2,
                         rwkv_k_k, rwkv_k_a, rwkv_r_k, rwkv_ln_w, rwkv_ln_b,
                         hgrn_lower_bounds, hgrn_norm_w, norm_ffn, w_gate, w_up, w_down, norm_final)
    b = x_prompt.shape[0]
    zero_states = tuple(jnp.zeros((1, b) + s, F32) for s in STATE_SHAPES)
    y_prompt, (gdn_p, conv_p, rwkv_p, shift_p, hgrn_p) = _trunk(x_prompt, zero_states, False, W)
    sample_states = (state_gdn, state_gdn_conv, state_rwkv, state_rwkv_shift[:, :, None, :], state_hgrn)
    y_sample, (gdn_s, conv_s, rwkv_s, shift_s, hgrn_s) = _trunk(x_sample, sample_states, True, W)
    return (y_prompt, y_sample, gdn_p, gdn_s, conv_p, conv_s, rwkv_p, rwkv_s, shift_p, shift_s, hgrn_p, hgrn_s)
```

```python
import functools

import jax
import jax.numpy as jnp
from jax import lax
from jax.experimental import pallas as pl
from jax.experimental.pallas import tpu as pltpu

D_MODEL = 1024
DEPTH = 4
HEAD_DIM = 64
A_WIDTH = 384
B_WIDTH = 384
C_WIDTH = 256
H_A = A_WIDTH // HEAD_DIM
H_B = B_WIDTH // HEAD_DIM
H_C = C_WIDTH // HEAD_DIM
CONV_W = 4
W_LORA = 64
A_LORA = 64
G_LORA = 128
V_LORA = 32
A_COLS = 4 * A_WIDTH + 2 * H_A
B_COLS = 3 * B_WIDTH + W_LORA + A_LORA + G_LORA
C_COLS = 4 * C_WIDTH
D_FF = 2816
RMS_EPS = 1e-6
RWKV_LN_EPS = 64e-5

LANES = 128
PAIR = 2 * HEAD_DIM
OFF_QKV = 0
OFF_Z = 3 * A_WIDTH
OFF_B = 4 * A_WIDTH
OFF_C = OFF_B + B_COLS
OFF_BA = OFF_C + C_COLS
P_COLS = OFF_BA + LANES
GATE_LANE_BETA = 0
GATE_LANE_A = H_A

PROMPT_CHUNK = 64
PROMPT_GROUP = 2
DECODE_ROWS = 8
DECODE_GROUP = 1
CARRY_ROWS = 8

STATE_SHAPES = ((H_A, HEAD_DIM, HEAD_DIM), (CONV_W - 1, 3 * A_WIDTH), (H_B, HEAD_DIM, HEAD_DIM),
                (1, B_COLS), (H_C, HEAD_DIM, HEAD_DIM))

BF = jnp.bfloat16
F32 = jnp.float32


def _mm(a, b):
    return jnp.dot(a.astype(BF), b.astype(BF), preferred_element_type=F32)


def _mm_nt(a, b):
    return lax.dot_general(a.astype(BF), b.astype(BF), (((1,), (1,)), ((), ())),
                           preferred_element_type=F32)


def _mm_tn(a, b):
    return lax.dot_general(a.astype(BF), b.astype(BF), (((0,), (0,)), ((), ())),
                           preferred_element_type=F32)


def _split3(x):
    x1 = x.astype(BF)
    r1 = x - x1.astype(F32)
    x2 = r1.astype(BF)
    x3 = (r1 - x2.astype(F32)).astype(BF)
    return x1, x2, x3


def _split2(x):
    x1 = x.astype(BF)
    x2 = (x - x1.astype(F32)).astype(BF)
    return x1, x2


def _mm_sel(sel, x, parts=_split2):
    s = sel.astype(BF)
    out = None
    for xp in parts(x):
        d = jnp.dot(s, xp, preferred_element_type=F32)
        out = d if out is None else out + d
    return out


def _mm_sel_r(x, sel, parts=_split2):
    s = sel.astype(BF)
    out = None
    for xp in parts(x):
        d = jnp.dot(xp, s, preferred_element_type=F32)
        out = d if out is None else out + d
    return out


def _colsum_bcast(x):
    ones = jnp.ones((x.shape[0], LANES), BF)
    dn = (((0,), (0,)), ((), ()))
    out = None
    for xp in _split2(x):
        d = lax.dot_general(xp, ones, dn, preferred_element_type=F32)
        out = d if out is None else out + d
    return out


def _head_sums(x):
    n = x.shape[-1]
    r = lax.broadcasted_iota(jnp.int32, (n, n), 0) // HEAD_DIM
    c = lax.broadcasted_iota(jnp.int32, (n, n), 1) // HEAD_DIM
    return _mm_sel_r(x, jnp.where(r == c, 1.0, 0.0))


def _sigmoid(x):
    return 1.0 / (1.0 + jnp.exp(-x))


def _silu(x):
    return x * _sigmoid(x)


def _softplus(x):
    return jnp.maximum(x, 0.0) + jnp.log1p(jnp.exp(-jnp.abs(x)))


def _rmsnorm(x, w):
    return x * lax.rsqrt(jnp.mean(x * x, -1, keepdims=True) + RMS_EPS) * w


def _lockstep(gens):
    gens = list(gens)
    while gens:
        alive = []
        for g in gens:
            try:
                next(g)
                alive.append(g)
            except StopIteration:
                pass
        gens = alive
        if gens:
            yield


def _inproj_kernel(x_ref, nw_ref, w_ref, o_ref):
    h = _rmsnorm(x_ref[...], nw_ref[...]).astype(BF)
    step = 512
    for c in range(0, P_COLS, step):
        o_ref[:, c:c + step] = jnp.dot(h, w_ref[:, c:c + step], preferred_element_type=F32)


def _inproj(x, norm_w, w):
    n = x.shape[0]
    tm = min(512, n)
    assert n % tm == 0
    return pl.pallas_call(
        _inproj_kernel,
        grid=(n // tm,),
        in_specs=[
            pl.BlockSpec((tm, D_MODEL), lambda i: (i, 0)),
            pl.BlockSpec((1, D_MODEL), lambda i: (0, 0)),
            pl.BlockSpec((D_MODEL, P_COLS), lambda i: (0, 0)),
        ],
        out_specs=pl.BlockSpec((tm, P_COLS), lambda i: (i, 0)),
        out_shape=jax.ShapeDtypeStruct((n, P_COLS), F32),
        compiler_params=pltpu.CompilerParams(dimension_semantics=("arbitrary",)),
        name="inproj",
    )(x, norm_w, w)


def _ffn_kernel(final, x_ref, mix_ref, wo_ref, nf_ref, wg_ref, wu_ref, wd_ref, nfin_ref, o_ref):
    x = x_ref[...] + jnp.dot(mix_ref[...], wo_ref[...], preferred_element_type=F32)
    h = _rmsnorm(x, nf_ref[...]).astype(BF)
    half = D_FF // 2
    acc = x
    for c in range(0, D_FF, half):
        gate = jnp.dot(h, wg_ref[:, c:c + half], preferred_element_type=F32)
        up = jnp.dot(h, wu_ref[:, c:c + half], preferred_element_type=F32)
        u = (_silu(gate) * up).astype(BF)
        acc = acc + jnp.dot(u, wd_ref[c:c + half, :], preferred_element_type=F32)
    if final:
        acc = _rmsnorm(acc, nfin_ref[...])
    o_ref[...] = acc


def _ffn(x, mix, w_out, norm_ffn, w_gate, w_up, w_down, norm_final, final):
    n = x.shape[0]
    tm = min(256, n)
    assert n % tm == 0
    const = lambda i: (0, 0)
    return pl.pallas_call(
        functools.partial(_ffn_kernel, final),
        grid=(n // tm,),
        in_specs=[
            pl.BlockSpec((tm, D_MODEL), lambda i: (i, 0)),
            pl.BlockSpec((tm, D_MODEL), lambda i: (i, 0)),
            pl.BlockSpec((D_MODEL, D_MODEL), const),
            pl.BlockSpec((1, D_MODEL), const),
            pl.BlockSpec((D_MODEL, D_FF), const),
            pl.BlockSpec((D_MODEL, D_FF), const),
            pl.BlockSpec((D_FF, D_MODEL), const),
            pl.BlockSpec((1, D_MODEL), const),
        ],
        out_specs=pl.BlockSpec((tm, D_MODEL), lambda i: (i, 0)),
        out_shape=jax.ShapeDtypeStruct((n, D_MODEL), F32),
        compiler_params=pltpu.CompilerParams(dimension_semantics=("arbitrary",),
                                             vmem_limit_bytes=56 * 1024 * 1024),
        name="ffn",
    )(x, mix, w_out, norm_ffn, w_gate, w_up, w_down, norm_final)


def _mixer_kernel(layer, rows, single, group, *refs):
    C = rows
    assert single or C == HEAD_DIM
    it = iter(refs)
    p_ref = next(it)
    vf_in_ref = next(it) if layer > 0 else None
    gdn_s0, conv0, rwkv_s0, shift0, hgrn_s0 = (next(it) for _ in range(5))
    conv_w, alog, dtb, gdn_nw = (next(it) for _ in range(4))
    mu, w0, w2, a0, a2, g2 = (next(it) for _ in range(6))
    if layer > 0:
        v0, v1, v2 = (next(it) for _ in range(3))
    kk_w, ka_w, rk_w, lnw, lnb = (next(it) for _ in range(5))
    hlb, hgrn_nw = (next(it) for _ in range(2))
    for _ in STATE_SHAPES:
        next(it)
    mix_ref = next(it)
    vf_out_ref = next(it) if layer == 0 else None
    gdn_s, conv_s, rwkv_s, shift_s, hgrn_s = (next(it) for _ in range(5))
    xc_ref, pbx_ref, gdn_bd, rwkv_bd, hgrn_bd = (next(it) for _ in range(5))

    t = pl.program_id(1)
    last_t = pl.num_programs(1) - 1
    keep = CONV_W - 1
    c0 = CARRY_ROWS - keep
    s0 = CARRY_ROWS - 1
    t_valid = 1 if single else C
    lo, hi = slice(0, HEAD_DIM), slice(HEAD_DIM, PAIR)
    state_io = ((gdn_s0, gdn_s, gdn_bd, H_A), (rwkv_s0, rwkv_s, rwkv_bd, H_B), (hgrn_s0, hgrn_s, hgrn_bd, H_C))

    def load_states():
        for src, _, bd, n_heads in state_io:
            bd[...] = jnp.zeros(bd.shape, F32)
            for g in range(group):
                for j in range(n_heads // 2):
                    bd[g, j, lo, lo] = src[g, 2 * j]
                    bd[g, j, hi, hi] = src[g, 2 * j + 1]
        xc_ref[:, c0:CARRY_ROWS, :] = conv0[...]
        pbx_ref[:, s0:CARRY_ROWS, :] = shift0[...]

    def store_states():
        for _, dst, bd, n_heads in state_io:
            for g in range(group):
                for j in range(n_heads // 2):
                    dst[g, 2 * j] = bd[g, j, lo, lo]
                    dst[g, 2 * j + 1] = bd[g, j, hi, hi]

    if single:
        load_states()
    else:
        pl.when(t == 0)(load_states)

    row = lax.broadcasted_iota(jnp.int32, (C, C), 0)
    col = lax.broadcasted_iota(jnp.int32, (C, C), 1)
    tri = jnp.where(col <= row, 1.0, 0.0).astype(F32)
    row2 = lax.broadcasted_iota(jnp.int32, (C, 2 * C), 0)
    col2 = lax.broadcasted_iota(jnp.int32, (C, 2 * C), 1) & (C - 1)
    def as01(cond):
        return jnp.where(cond, 1.0, 0.0).astype(F32)

    causal2 = as01(col2 <= row2)
    strict2 = as01(col2 < row2)
    eye2 = as01(col2 == row2)
    mask_ar2 = jnp.concatenate([strict2, causal2], axis=0)
    lane_lo = lax.broadcasted_iota(jnp.int32, (1, PAIR), 1) < HEAD_DIM
    on_lo = as01(lane_lo)
    on_hi = as01(~lane_lo)
    bd_mask = as01((lax.broadcasted_iota(jnp.int32, (PAIR, PAIR), 0) < HEAD_DIM)
                   == (lax.broadcasted_iota(jnp.int32, (PAIR, PAIR), 1) < HEAD_DIM))
    if single:
        valid = lax.broadcasted_iota(jnp.int32, (C, 1), 0) < 1

    def mask_rows(x):
        return jnp.where(valid, x, 0.0) if single else x

    def load_rows(ref, g, start, width):
        x = ref[g, :, start:start + width]
        return jnp.broadcast_to(x, (C, width)) if single else x

    def store_rows(ref, g, start, val):
        val = val[0:1] if single else val
        ref[g, :, start:start + val.shape[-1]] = val.astype(ref.dtype)

    def cumsum_rows(x, parts=_split2):
        if single:
            return jnp.broadcast_to(x[0:1], x.shape)
        return _mm_sel(tri, x, parts)

    def rbd(x):
        return jnp.concatenate([x * on_lo, x * on_hi], axis=0).astype(BF)

    def pair_cols(x):
        if C == HEAD_DIM:
            return x
        return jnp.concatenate([x[:, :C], x[:, HEAD_DIM:HEAD_DIM + C]], axis=1)

    def pair(x, j):
        return x[:, j * PAIR:(j + 1) * PAIR]

    def run_solve(pp, tiles):
        span = 1
        while True:
            pp_bf = pp.astype(BF)
            his = [x.astype(BF).astype(F32) for x in tiles]
            r_hi = jnp.concatenate([rbd(h) for h in his], axis=1)
            r_lo = jnp.concatenate([rbd(x - h) for x, h in zip(tiles, his)], axis=1)
            pr = (jnp.dot(pp_bf, r_hi, preferred_element_type=F32)
                  + jnp.dot(pp_bf, r_lo, preferred_element_type=F32))
            span *= 2
            pp_next = jnp.dot(pp_bf, rbd(pp), preferred_element_type=F32) if span < C else None
            yield
            tiles = [x + pr[:, n * PAIR:(n + 1) * PAIR] for n, x in enumerate(tiles)]
            if span >= C:
                return tiles
            pp = pp_next


    def gdn_mixer(g):
        xc_ref[g, CARRY_ROWS:CARRY_ROWS + C, :] = load_rows(p_ref, g, OFF_QKV, 3 * A_WIDTH)
        cw = conv_w[...]
        y = xc_ref[g, c0:c0 + C, :] * cw[0:1]
        for j in range(1, CONV_W):
            y = y + xc_ref[g, c0 + j:c0 + j + C, :] * cw[j:j + 1]
        new_buf = xc_ref[g, c0 + t_valid:c0 + t_valid + keep, :]
        conv_s[g] = new_buf
        xc_ref[g, c0:CARRY_ROWS, :] = new_buf
        qkv = _silu(y)
        q_all = qkv[:, 0:A_WIDTH]
        k_all = qkv[:, A_WIDTH:2 * A_WIDTH]
        v_all = mask_rows(qkv[:, 2 * A_WIDTH:3 * A_WIDTH])
        z_all = load_rows(p_ref, g, OFF_Z, A_WIDTH)
        ba = load_rows(p_ref, g, OFF_BA, LANES)
        beta_all = mask_rows(_sigmoid(ba))
        g_all = mask_rows(-jnp.exp(alog[...]) * _softplus(ba + dtb[...]))
        gcum = cumsum_rows(g_all, _split3)
        q_ss = _head_sums(q_all * q_all)
        k_ss = _head_sums(k_all * k_all)
        yield
        src_lane = lax.broadcasted_iota(jnp.int32, (LANES, A_WIDTH), 0)
        dst_head = lax.broadcasted_iota(jnp.int32, (LANES, A_WIDTH), 1) // HEAD_DIM
        beta_x = _mm_sel_r(beta_all, jnp.where(src_lane == dst_head + GATE_LANE_BETA, 1.0, 0.0))
        gcum_x = _mm_sel_r(gcum, jnp.where(src_lane == dst_head + GATE_LANE_A, 1.0, 0.0))
        q_all = q_all * lax.rsqrt(q_ss + 1e-6) * HEAD_DIM ** -0.5
        k_all = mask_rows(k_all * lax.rsqrt(k_ss + 1e-6))
        yield
        if not single:
            r6 = lax.broadcasted_iota(jnp.int32, (C, A_WIDTH), 0)
            c6 = lax.broadcasted_iota(jnp.int32, (C, A_WIDTH), 1) & (HEAD_DIM - 1)
            gr_x = _mm_sel(jnp.ones((C, C), F32), jnp.where(c6 == r6, gcum_x, 0.0))
            yield
            decay_x = jnp.exp(jnp.minimum(gcum_x - gr_x, 0.0)) * as01(c6 <= r6)
        egc_x = jnp.exp(gcum_x)
        gl_x = gcum_x[C - 1:C, :]
        kdec_x = k_all * jnp.exp(gl_x - gcum_x)
        u0_x = v_all * beta_x
        w0_x = k_all * (beta_x * egc_x)
        qe_x = q_all * egc_x
        o_tiles = [None] * (H_A // 2)

        def head_pair(j):
            k = pair(k_all, j)
            decay = causal2 if single else pair(decay_x, j)
            krb = rbd(k)
            if not single:
                kk = _mm_nt(k, krb)
            qk = _mm_nt(pair(q_all, j), krb)
            qs = _mm(pair(qe_x, j), gdn_bd[g, j])
            yield
            if single:
                u, w = pair(u0_x, j), pair(w0_x, j)
            else:
                neg_m = -(pair(beta_x, j) * kk * decay) * strict2
                u, w = yield from run_solve(neg_m, [pair(u0_x, j), pair(w0_x, j)])
            ws = _mm(w, gdn_bd[g, j])
            yield
            v_new = u - ws
            o2 = _mm(qk * decay, rbd(v_new))
            ds = _mm_tn(pair(kdec_x, j), v_new)
            yield
            o_tiles[j] = qs + o2
            gdn_bd[g, j] = gdn_bd[g, j] * jnp.exp(pair(gl_x, j)) + ds * bd_mask

        yield from _lockstep([head_pair(j) for j in range(H_A // 2)])
        o = jnp.concatenate(o_tiles, axis=1)
        o_ss = _head_sums(o * o)
        yield
        o = o * lax.rsqrt(o_ss * (1.0 / HEAD_DIM) + RMS_EPS) * gdn_nw[...] * _silu(z_all)
        store_rows(mix_ref, g, 0, o)

    def rwkv_mixer(g):
        pb = load_rows(p_ref, g, OFF_B, B_COLS)
        pbx_ref[g, CARRY_ROWS:CARRY_ROWS + C, :] = pb
        prev = pbx_ref[g, s0:s0 + C, :]
        new_shift = pbx_ref[g, s0 + t_valid:s0 + t_valid + 1, :]
        shift_s[g] = new_shift
        pbx_ref[g, s0:CARRY_ROWS, :] = new_shift
        xs = pb + (prev - pb) * mu[...]
        r_all = xs[:, 0:B_WIDTH]
        k_all = xs[:, B_WIDTH:2 * B_WIDTH]
        v_all = xs[:, 2 * B_WIDTH:3 * B_WIDTH]
        o_w = 3 * B_WIDTH
        wd = xs[:, o_w:o_w + W_LORA]
        ad = xs[:, o_w + W_LORA:o_w + W_LORA + A_LORA]
        gd = xs[:, o_w + W_LORA + A_LORA:o_w + W_LORA + A_LORA + G_LORA]
        w_lora = _mm(jnp.tanh(wd), w2[...])
        a_lora = _mm(ad, a2[...])
        g_out = _mm(_sigmoid(gd), g2[...])
        kkx = k_all * kk_w[...]
        kk_ss = _head_sums(kkx * kkx)
        if layer == 0:
            store_rows(vf_out_ref, g, 0, v_all)
        else:
            v_lora = _mm(v_all, v1[...])
        yield
        w_log = -_softplus(-(w0[...] + w_lora)) - 0.5
        log_decay = mask_rows(-jnp.exp(w_log))
        cum = cumsum_rows(log_decay)
        a_all = _sigmoid(a0[...] + a_lora)
        k_all = mask_rows(k_all * (1.0 + (a_all - 1.0) * ka_w[...]))
        rk_ss = _head_sums(r_all * k_all * rk_w[...])
        if layer > 0:
            v_mix = _mm(v_lora, v2[...])
        yield
        if layer > 0:
            mixv = _sigmoid(v0[...] + v_mix)
            v_all = v_all + (load_rows(vf_in_ref, g, 0, B_WIDTH) - v_all) * mixv
        v_all = mask_rows(v_all)
        kk = mask_rows(kkx * lax.rsqrt(kk_ss + 1e-6))
        bh = kk * a_all
        cl = cum[C - 1:C, :]
        einv = jnp.exp(-cum)
        ehat = jnp.exp(cl - cum)
        ar_x = jnp.concatenate([-kk * jnp.exp(cum - log_decay), r_all * jnp.exp(cum)], axis=0)
        bt_x = bh * einv
        kt_x = k_all * einv
        bk_x = jnp.concatenate([bh * ehat, k_all * ehat], axis=0)
        ecl_x = jnp.exp(cl)
        y_tiles = [None] * (H_B // 2)

        def head_pair(j):
            ar = pair(ar_x, j)
            v = pair(v_all, j)
            gb = _mm_nt(ar, rbd(pair(bt_x, j)))
            gk = _mm_nt(ar, rbd(pair(kt_x, j)))
            ars = _mm_nt(ar, rwkv_bd[g, j])
            yield
            gbm = gb * mask_ar2
            gkm = gk * mask_ar2
            lmv = _mm(gkm, rbd(v))
            yield
            if single:
                u = ars[:C]
            else:
                (u,) = yield from run_solve(gbm[:C], [ars[:C] + lmv[:C]])
            mrbu = _mm(gbm[C:], rbd(u))
            ds = _mm_tn(jnp.concatenate([u, v], axis=0), pair(bk_x, j))
            yield
            y_tiles[j] = ars[C:] + mrbu + lmv[C:]
            rwkv_bd[g, j] = rwkv_bd[g, j] * pair(ecl_x, j) + ds * bd_mask

        yield from _lockstep([head_pair(j) for j in range(H_B // 2)])
        yh = jnp.concatenate(y_tiles, axis=1)
        mean = _head_sums(yh) * (1.0 / HEAD_DIM)
        yield
        yc = yh - mean
        var = _head_sums(yc * yc) * (1.0 / HEAD_DIM)
        yield
        yn = yc * lax.rsqrt(var + RWKV_LN_EPS) * lnw[...] + lnb[...]
        out = (yn + rk_ss * v_all) * g_out
        store_rows(mix_ref, g, A_WIDTH, out)

    def hgrn_mixer(g):
        pc = load_rows(p_ref, g, OFF_C, C_COLS)
        q_all = _silu(pc[:, 0:C_WIDTH]) * HEAD_DIM ** -0.5
        f_all = pc[:, C_WIDTH:2 * C_WIDTH]
        i_all = pc[:, 2 * C_WIDTH:3 * C_WIDTH]
        go_all = pc[:, 3 * C_WIDTH:4 * C_WIDTH]
        hl = hlb[...]
        hmax = jnp.max(hl, axis=0, keepdims=True)
        he = jnp.exp(hl - hmax)
        sm = he / jnp.sum(he, axis=0, keepdims=True)
        lb_acc = sm[0:1]
        for mrow in range(1, layer + 1):
            lb_acc = lb_acc + sm[mrow:mrow + 1]
        lb = lb_acc - sm[0:1]
        k_all = mask_rows((1.0 - lb) * _sigmoid(-f_all))
        log_f = jnp.log1p(-k_all)
        lsel = []
        pair_masks = []
        levels = []
        if not single:
            ls = 0
            while (1 << ls) < C:
                levels.append(ls)
                ls += 1
        for ls in levels:
            blk = row >> ls
            start = blk << ls
            odd = (blk & 1) == 1
            sel_q = odd & (col >= start) & (col <= row)
            sel_k = (~odd) & (col > row) & (col <= start + ((1 << ls) - 1))
            lsel.append(jnp.where(sel_q | sel_k, 1.0, 0.0).astype(F32))
            blk2 = row2 >> ls
            pair_masks.append(as01(((blk2 & 1) == 1) & ((col2 >> ls) == blk2 - 1)))
        bcum = cumsum_rows(log_f)
        if levels:
            lvl_sums = _mm_sel(jnp.concatenate(lsel, axis=0), log_f)
        col_sums = _colsum_bcast(log_f)
        qk_ss = _head_sums(q_all * k_all)
        yield
        if levels:
            e_lvls = jnp.exp(lvl_sums)
        qe_x = q_all * jnp.exp(bcum)
        k_hat = k_all * jnp.exp(bcum[C - 1:C, :] - bcum)
        e_col = jnp.exp(col_sums)
        o_tiles = [None] * (H_C // 2)

        def head_pair(j):
            q = pair(q_all, j)
            k = pair(k_all, j)
            v = pair(i_all, j)
            parts = []
            for n in range(len(levels)):
                e = pair(e_lvls[n * C:(n + 1) * C], j)
                parts.append(_mm_nt(q * e, rbd(k * e)))
            qs = _mm(pair(qe_x, j), hgrn_bd[g, j])
            ds = _mm_tn(pair(k_hat, j), v)
            yield
            att = eye2 * pair_cols(pair(qk_ss, j))
            for n in range(len(levels)):
                att = att + pair_masks[n] * parts[n]
            o2 = _mm(att, rbd(v))
            yield
            o_tiles[j] = qs + o2
            hgrn_bd[g, j] = (e_col[j * PAIR:(j + 1) * PAIR, :] * hgrn_bd[g, j]
                             + ds * bd_mask)

        yield from _lockstep([head_pair(j) for j in range(H_C // 2)])
        o = jnp.concatenate(o_tiles, axis=1)
        o_ss = _head_sums(o * o)
        yield
        o = o * lax.rsqrt(o_ss * (1.0 / HEAD_DIM) + RMS_EPS) * hgrn_nw[...] * _silu(go_all)
        store_rows(mix_ref, g, A_WIDTH + B_WIDTH, o)

    mixers = []
    for g in range(group):
        mixers += [gdn_mixer(g), rwkv_mixer(g), hgrn_mixer(g)]
    for _ in _lockstep(mixers):
        pass

    if single:
        store_states()
    else:
        pl.when(t == last_t)(store_states)


def _mixers(layer, p, vfirst, states_in, state_layer, acc, lp, rows, single, group):
    bsz, t, _ = p.shape
    t_blk = 1 if single else rows
    nt = t // t_blk
    assert nt * t_blk == t and bsz % group == 0 and (not single or t == 1)

    def per_bt(width):
        return pl.BlockSpec((group, t_blk, width), lambda b, t_: (b, t_, 0))

    def per_b(shape, lyr):
        nd = len(shape)
        return pl.BlockSpec((None, group) + shape, lambda b, t_: (lyr, b) + (0,) * nd)

    def const(arr):
        nd = arr.ndim
        return pl.BlockSpec(arr.shape, lambda b, t_: (0,) * nd)

    inputs = [p]
    in_specs = [per_bt(P_COLS)]
    if layer > 0:
        inputs.append(vfirst)
        in_specs.append(per_bt(B_WIDTH))
    inputs += list(states_in)
    in_specs += [per_b(s, state_layer) for s in STATE_SHAPES]
    names = ["conv_w", "alog", "dtb", "gdn_nw", "mu", "w0", "w2", "a0", "a2", "g2"]
    if layer > 0:
        names += ["v0", "v1", "v2"]
    names += ["kk_w", "ka_w", "rk_w", "lnw", "lnb", "hlb", "hgrn_nw"]
    for nm in names:
        inputs.append(lp[nm])
        in_specs.append(const(lp[nm]))
    acc_first = len(inputs)
    inputs += list(acc)
    in_specs += [pl.BlockSpec(memory_space=pl.ANY) for _ in acc]

    out_shape = [jax.ShapeDtypeStruct((bsz, t, D_MODEL), BF)]
    out_specs = [per_bt(D_MODEL)]
    if layer == 0:
        out_shape.append(jax.ShapeDtypeStruct((bsz, t, B_WIDTH), F32))
        out_specs.append(per_bt(B_WIDTH))
    state_first = len(out_shape)
    for s, a in zip(STATE_SHAPES, acc):
        out_shape.append(jax.ShapeDtypeStruct(a.shape, F32))
        out_specs.append(per_b(s, layer))
    aliases = {acc_first + i: state_first + i for i in range(len(acc))}

    outs = pl.pallas_call(
        functools.partial(_mixer_kernel, layer, rows, single, group),
        grid=(bsz // group, nt),
        in_specs=in_specs,
        out_specs=out_specs,
        out_shape=out_shape,
        input_output_aliases=aliases,
        scratch_shapes=[pltpu.VMEM((group, CARRY_ROWS + rows, 3 * A_WIDTH), F32),
                        pltpu.VMEM((group, CARRY_ROWS + rows, B_COLS), F32),
                        pltpu.VMEM((group, H_A // 2, PAIR, PAIR), F32),
                        pltpu.VMEM((group, H_B // 2, PAIR, PAIR), F32),
                        pltpu.VMEM((group, H_C // 2, PAIR, PAIR), F32)],
        compiler_params=pltpu.CompilerParams(dimension_semantics=("arbitrary", "arbitrary")),
        name=f"mixers_l{layer}_r{rows}",
    )(*inputs)
    mix = outs[0]
    if layer == 0:
        vfirst = outs[1]
    return mix, vfirst, tuple(outs[state_first:])


def _trunk(x, states_in, layered_states, W):
    bsz, t, _ = x.shape
    if t == 1:
        rows, single, group = DECODE_ROWS, True, DECODE_GROUP
    else:
        assert t % PROMPT_CHUNK == 0
        rows, single, group = PROMPT_CHUNK, False, PROMPT_GROUP
    n = bsz * t
    xf = x.reshape(n, D_MODEL)
    acc = tuple(jnp.zeros((DEPTH, bsz) + s, F32) for s in STATE_SHAPES)
    vfirst = None
    for l in range(DEPTH):
        lp = W["layers"][l]
        p = _inproj(xf, lp["norm_mix"], lp["w_in"]).reshape(bsz, t, P_COLS)
        mix, vfirst, acc = _mixers(l, p, vfirst, states_in, l if layered_states else 0, acc, lp,
                                   rows, single, group)
        xf = _ffn(xf, mix.reshape(n, D_MODEL), lp["w_out"], lp["norm_ffn"], lp["w_gate"], lp["w_up"],
                  lp["w_down"], W["norm_final"], final=(l == DEPTH - 1))
    y = xf.reshape(bsz, t, D_MODEL)
    gdn, conv, rwkv, shift, hgrn = acc
    return y, (gdn, conv, rwkv, shift[:, :, 0, :], hgrn)


def _prepare_weights(norm_mix, w_in, w_out, gdn_conv_w, gdn_a_log, gdn_dt_bias, gdn_norm_w,
                     rwkv_mu, rwkv_w0, rwkv_w2, rwkv_a0, rwkv_a2, rwkv_g2, rwkv_v0, rwkv_v1, rwkv_v2,
                     rwkv_k_k, rwkv_k_a, rwkv_r_k, rwkv_ln_w, rwkv_ln_b,
                     hgrn_lower_bounds, hgrn_norm_w, norm_ffn, w_gate, w_up, w_down, norm_final):
    depth = w_in.shape[0]
    ba_cols = w_in[:, :, 4 * A_WIDTH:A_COLS]
    pad = jnp.zeros((depth, D_MODEL, LANES - 2 * H_A), w_in.dtype)
    w_in_r = jnp.concatenate([w_in[:, :, :4 * A_WIDTH], w_in[:, :, A_COLS:], ba_cols, pad], axis=-1).astype(BF)

    def gate_row(v):
        z_lo = jnp.zeros((depth, GATE_LANE_A), F32)
        z_hi = jnp.zeros((depth, LANES - GATE_LANE_A - H_A), F32)
        return jnp.concatenate([z_lo, v.astype(F32), z_hi], axis=-1)[:, None, :]

    alog = gate_row(gdn_a_log)
    dtb = gate_row(gdn_dt_bias)
    layers = []
    for l in range(depth):
        lp = {
            "norm_mix": norm_mix[l][None], "w_in": w_in_r[l], "w_out": w_out[l].astype(BF),
            "norm_ffn": norm_ffn[l][None], "w_gate": w_gate[l].astype(BF), "w_up": w_up[l].astype(BF),
            "w_down": w_down[l].astype(BF),
            "conv_w": gdn_conv_w[l], "alog": alog[l], "dtb": dtb[l],
            "gdn_nw": jnp.tile(gdn_norm_w[l], H_A)[None],
            "mu": rwkv_mu[l][None], "w0": rwkv_w0[l][None], "w2": rwkv_w2[l], "a0": rwkv_a0[l][None],
            "a2": rwkv_a2[l], "g2": rwkv_g2[l],
            "kk_w": rwkv_k_k[l][None], "ka_w": rwkv_k_a[l][None], "rk_w": rwkv_r_k[l].reshape(1, B_WIDTH),
            "lnw": rwkv_ln_w[l][None], "lnb": rwkv_ln_b[l][None],
            "hlb": hgrn_lower_bounds, "hgrn_nw": jnp.tile(hgrn_norm_w[l], H_C)[None],
        }
        if l > 0:
            lp.update(v0=rwkv_v0[l - 1][None], v1=rwkv_v1[l - 1], v2=rwkv_v2[l - 1])
        layers.append(lp)
    return {"layers": layers, "norm_final": norm_final[None]}


def kernel(x_prompt, x_sample, state_gdn, state_gdn_conv, state_rwkv, state_rwkv_shift, state_hgrn, norm_mix, w_in, w_out, gdn_conv_w, gdn_a_log, gdn_dt_bias, gdn_norm_w, rwkv_mu, rwkv_w0, rwkv_w2, rwkv_a0, rwkv_a2, rwkv_g2, rwkv_v0, rwkv_v1, rwkv_v2, rwkv_k_k, rwkv_k_a, rwkv_r_k, rwkv_ln_w, rwkv_ln_b, hgrn_lower_bounds, hgrn_norm_w, norm_ffn, w_gate, w_up, w_down, norm_final):
    W = _prepare_weights(norm_mix, w_in, w_out, gdn_conv_w, gdn_a_log, gdn_dt_bias, gdn_norm_w,
                         rwkv_mu, rwkv_w0, rwkv_w2, rwkv_a0, rwkv_a2, rwkv_g2, rwkv_v0, rwkv_v1, rwkv_v2,
                         rwkv_k_k, rwkv_k_a, rwkv_r_k, rwkv_ln_w, rwkv_ln_b,
                         hgrn_lower_bounds, hgrn_norm_w, norm_ffn, w_gate, w_up, w_down, norm_final)
    b = x_prompt.shape[0]
    zero_states = tuple(jnp.zeros((1, b) + s, F32) for s in STATE_SHAPES)
    y_prompt, (gdn_p, conv_p, rwkv_p, shift_p, hgrn_p) = _trunk(x_prompt, zero_states, False, W)
    sample_states = (state_gdn, state_gdn_conv, state_rwkv, state_rwkv_shift[:, :, None, :], state_hgrn)
    y_sample, (gdn_s, conv_s, rwkv_s, shift_s, hgrn_s) = _trunk(x_sample, sample_states, True, W)
    return (y_prompt, y_sample, gdn_p, gdn_s, conv_p, conv_s, rwkv_p, rwkv_s, shift_p, shift_s, hgrn_p, hgrn_s)
```

```python
import functools

import jax
import jax.numpy as jnp
from jax import lax
from jax.experimental import pallas as pl
from jax.experimental.pallas import tpu as pltpu

D_MODEL = 1024
DEPTH = 4
HEAD_DIM = 64
A_WIDTH = 384
B_WIDTH = 384
C_WIDTH = 256
H_A = A_WIDTH // HEAD_DIM
H_B = B_WIDTH // HEAD_DIM
H_C = C_WIDTH // HEAD_DIM
CONV_W = 4
W_LORA = 64
A_LORA = 64
G_LORA = 128
V_LORA = 32
A_COLS = 4 * A_WIDTH + 2 * H_A
B_COLS = 3 * B_WIDTH + W_LORA + A_LORA + G_LORA
C_COLS = 4 * C_WIDTH
D_FF = 2816
RMS_EPS = 1e-6
RWKV_LN_EPS = 64e-5

LANES = 128
PAIR = 2 * HEAD_DIM
OFF_QKV = 0
OFF_Z = 3 * A_WIDTH
OFF_B = 4 * A_WIDTH
OFF_C = OFF_B + B_COLS
OFF_BA = OFF_C + C_COLS
P_COLS = OFF_BA + LANES
GATE_LANE_BETA = 0
GATE_LANE_A = H_A

PROMPT_CHUNK = 64
PROMPT_GROUP = 2
GROUP_STAGGER = 0
DECODE_ROWS = 8
DECODE_GROUP = 4
CARRY_ROWS = 8

STATE_SHAPES = ((H_A, HEAD_DIM, HEAD_DIM), (CONV_W - 1, 3 * A_WIDTH), (H_B, HEAD_DIM, HEAD_DIM),
                (1, B_COLS), (H_C, HEAD_DIM, HEAD_DIM))

BF = jnp.bfloat16
F32 = jnp.float32


def _mm(a, b):
    return jnp.dot(a.astype(BF), b.astype(BF), preferred_element_type=F32)


def _mm_nt(a, b):
    return lax.dot_general(a.astype(BF), b.astype(BF), (((1,), (1,)), ((), ())),
                           preferred_element_type=F32)


def _mm_tn(a, b):
    return lax.dot_general(a.astype(BF), b.astype(BF), (((0,), (0,)), ((), ())),
                           preferred_element_type=F32)


def _split3(x):
    x1 = x.astype(BF)
    r1 = x - x1.astype(F32)
    x2 = r1.astype(BF)
    x3 = (r1 - x2.astype(F32)).astype(BF)
    return x1, x2, x3


def _split2(x):
    x1 = x.astype(BF)
    x2 = (x - x1.astype(F32)).astype(BF)
    return x1, x2


def _mm_sel(sel, x, parts=_split2):
    s = sel.astype(BF)
    out = None
    for xp in parts(x):
        d = jnp.dot(s, xp, preferred_element_type=F32)
        out = d if out is None else out + d
    return out


def _mm_sel_r(x, sel, parts=_split2):
    s = sel.astype(BF)
    out = None
    for xp in parts(x):
        d = jnp.dot(xp, s, preferred_element_type=F32)
        out = d if out is None else out + d
    return out


def _colsum_bcast(x):
    ones = jnp.ones((x.shape[0], LANES), BF)
    dn = (((0,), (0,)), ((), ()))
    out = None
    for xp in _split2(x):
        d = lax.dot_general(xp, ones, dn, preferred_element_type=F32)
        out = d if out is None else out + d
    return out


def _head_sums(x):
    n = x.shape[-1]
    r = lax.broadcasted_iota(jnp.int32, (n, n), 0) // HEAD_DIM
    c = lax.broadcasted_iota(jnp.int32, (n, n), 1) // HEAD_DIM
    return _mm_sel_r(x, jnp.where(r == c, 1.0, 0.0))


def _sigmoid(x):
    return 1.0 / (1.0 + jnp.exp(-x))


def _silu(x):
    return x * _sigmoid(x)


def _softplus(x):
    return jnp.maximum(x, 0.0) + jnp.log1p(jnp.exp(-jnp.abs(x)))


def _rmsnorm(x, w):
    return x * lax.rsqrt(jnp.mean(x * x, -1, keepdims=True) + RMS_EPS) * w


def _lockstep(gens):
    gens = list(gens)
    while gens:
        alive = []
        for g in gens:
            try:
                next(g)
                alive.append(g)
            except StopIteration:
                pass
        gens = alive
        if gens:
            yield


def _inproj_kernel(x_ref, nw_ref, w_ref, o_ref):
    h = _rmsnorm(x_ref[...], nw_ref[...]).astype(BF)
    step = 512
    for c in range(0, P_COLS, step):
        o_ref[:, c:c + step] = jnp.dot(h, w_ref[:, c:c + step], preferred_element_type=F32)


def _inproj(x, norm_w, w, layer):
    n = x.shape[0]
    tm = min(512, n)
    assert n % tm == 0
    return pl.pallas_call(
        _inproj_kernel,
        grid=(n // tm,),
        in_specs=[
            pl.BlockSpec((tm, D_MODEL), lambda i: (i, 0)),
            pl.BlockSpec((1, D_MODEL), lambda i: (0, 0)),
            pl.BlockSpec((None, D_MODEL, P_COLS), lambda i: (layer, 0, 0)),
        ],
        out_specs=pl.BlockSpec((tm, P_COLS), lambda i: (i, 0)),
        out_shape=jax.ShapeDtypeStruct((n, P_COLS), F32),
        compiler_params=pltpu.CompilerParams(dimension_semantics=("arbitrary",)),
        name="inproj",
    )(x, norm_w, w)


def _ffn_kernel(final, x_ref, mix_ref, wo_ref, nf_ref, wg_ref, wu_ref, wd_ref, nfin_ref, o_ref):
    x = x_ref[...] + jnp.dot(mix_ref[...], wo_ref[...], preferred_element_type=F32)
    h = _rmsnorm(x, nf_ref[...]).astype(BF)
    half = D_FF // 2
    acc = x
    for c in range(0, D_FF, half):
        gate = jnp.dot(h, wg_ref[:, c:c + half], preferred_element_type=F32)
        up = jnp.dot(h, wu_ref[:, c:c + half], preferred_element_type=F32)
        u = (_silu(gate) * up).astype(BF)
        acc = acc + jnp.dot(u, wd_ref[c:c + half, :], preferred_element_type=F32)
    if final:
        acc = _rmsnorm(acc, nfin_ref[...])
    o_ref[...] = acc


def _ffn(x, mix, w_out, norm_ffn, w_gate, w_up, w_down, norm_final, layer, final):
    n = x.shape[0]
    tm = min(512, n)
    assert n % tm == 0
    const = lambda i: (0, 0)
    of_layer = lambda i: (layer, 0, 0)
    once = pl.Buffered(1)
    return pl.pallas_call(
        functools.partial(_ffn_kernel, final),
        grid=(n // tm,),
        in_specs=[
            pl.BlockSpec((tm, D_MODEL), lambda i: (i, 0)),
            pl.BlockSpec((tm, D_MODEL), lambda i: (i, 0)),
            pl.BlockSpec((None, D_MODEL, D_MODEL), of_layer, pipeline_mode=once),
            pl.BlockSpec((1, D_MODEL), const),
            pl.BlockSpec((None, D_MODEL, D_FF), of_layer, pipeline_mode=once),
            pl.BlockSpec((None, D_MODEL, D_FF), of_layer, pipeline_mode=once),
            pl.BlockSpec((None, D_FF, D_MODEL), of_layer, pipeline_mode=once),
            pl.BlockSpec((1, D_MODEL), const),
        ],
        out_specs=pl.BlockSpec((tm, D_MODEL), lambda i: (i, 0)),
        out_shape=jax.ShapeDtypeStruct((n, D_MODEL), F32),
        compiler_params=pltpu.CompilerParams(dimension_semantics=("arbitrary",),
                                             vmem_limit_bytes=56 * 1024 * 1024),
        name="ffn",
    )(x, mix, w_out, norm_ffn, w_gate, w_up, w_down, norm_final)


def _mixer_kernel(layer, rows, single, group, *refs):
    C = rows
    assert single or C == HEAD_DIM
    it = iter(refs)
    p_ref = next(it)
    vf_in_ref = next(it) if layer > 0 else None
    gdn_s0, conv0, rwkv_s0, shift0, hgrn_s0 = (next(it) for _ in range(5))
    conv_w, alog, dtb, gdn_nw = (next(it) for _ in range(4))
    mu, w0, w2, a0, a2, g2 = (next(it) for _ in range(6))
    if layer > 0:
        v0, v1, v2 = (next(it) for _ in range(3))
    kk_w, ka_w, rk_w, lnw, lnb = (next(it) for _ in range(5))
    hlb, hgrn_nw = (next(it) for _ in range(2))
    for _ in STATE_SHAPES:
        next(it)
    mix_ref = next(it)
    vf_out_ref = next(it) if layer == 0 else None
    gdn_s, conv_s, rwkv_s, shift_s, hgrn_s = (next(it) for _ in range(5))
    xc_ref, pbx_ref, gdn_bd, rwkv_bd, hgrn_bd = (next(it) for _ in range(5))

    t = pl.program_id(1)
    last_t = pl.num_programs(1) - 1
    keep = CONV_W - 1
    c0 = CARRY_ROWS - keep
    s0 = CARRY_ROWS - 1
    t_valid = 1 if single else C
    lo, hi = slice(0, HEAD_DIM), slice(HEAD_DIM, PAIR)
    state_io = ((gdn_s0, gdn_s, gdn_bd, H_A), (rwkv_s0, rwkv_s, rwkv_bd, H_B), (hgrn_s0, hgrn_s, hgrn_bd, H_C))

    def load_states():
        for src, _, bd, n_heads in state_io:
            bd[...] = jnp.zeros(bd.shape, F32)
            for g in range(group):
                for j in range(n_heads // 2):
                    bd[g, j, lo, lo] = src[g, 2 * j]
                    bd[g, j, hi, hi] = src[g, 2 * j + 1]
        xc_ref[:, c0:CARRY_ROWS, :] = conv0[...]
        pbx_ref[:, s0:CARRY_ROWS, :] = shift0[...]

    def store_states():
        for _, dst, bd, n_heads in state_io:
            for g in range(group):
                for j in range(n_heads // 2):
                    dst[g, 2 * j] = bd[g, j, lo, lo]
                    dst[g, 2 * j + 1] = bd[g, j, hi, hi]

    if single:
        load_states()
    else:
        pl.when(t == 0)(load_states)

    row = lax.broadcasted_iota(jnp.int32, (C, C), 0)
    col = lax.broadcasted_iota(jnp.int32, (C, C), 1)
    tri = jnp.where(col <= row, 1.0, 0.0).astype(F32)
    row2 = lax.broadcasted_iota(jnp.int32, (C, 2 * C), 0)
    col2 = lax.broadcasted_iota(jnp.int32, (C, 2 * C), 1) & (C - 1)
    def as01(cond):
        return jnp.where(cond, 1.0, 0.0).astype(F32)

    causal2 = as01(col2 <= row2)
    strict2 = as01(col2 < row2)
    eye2 = as01(col2 == row2)
    mask_ar2 = jnp.concatenate([strict2, causal2], axis=0)
    lane_lo = lax.broadcasted_iota(jnp.int32, (1, PAIR), 1) < HEAD_DIM
    on_lo = as01(lane_lo)
    on_hi = as01(~lane_lo)
    bd_mask = as01((lax.broadcasted_iota(jnp.int32, (PAIR, PAIR), 0) < HEAD_DIM)
                   == (lax.broadcasted_iota(jnp.int32, (PAIR, PAIR), 1) < HEAD_DIM))
    if single:
        valid = lax.broadcasted_iota(jnp.int32, (C, 1), 0) < 1

    def mask_rows(x):
        return jnp.where(valid, x, 0.0) if single else x

    def load_rows(ref, g, start, width):
        x = ref[g, :, start:start + width]
        return jnp.broadcast_to(x, (C, width)) if single else x

    def store_rows(ref, g, start, val):
        val = val[0:1] if single else val
        ref[g, :, start:start + val.shape[-1]] = val.astype(ref.dtype)

    def cumsum_rows(x, parts=_split2):
        if single:
            return jnp.broadcast_to(x[0:1], x.shape)
        return _mm_sel(tri, x, parts)

    def rbd(x):
        return jnp.concatenate([x * on_lo, x * on_hi], axis=0).astype(BF)

    def pair_cols(x):
        if C == HEAD_DIM:
            return x
        return jnp.concatenate([x[:, :C], x[:, HEAD_DIM:HEAD_DIM + C]], axis=1)

    def pair(x, j):
        return x[:, j * PAIR:(j + 1) * PAIR]

    strict_s = jnp.concatenate([as01(col < row)] * 2, axis=0)
    causal_s = jnp.concatenate([tri] * 2, axis=0)
    sel_hi = as01(lax.broadcasted_iota(jnp.int32, (PAIR, HEAD_DIM), 0)
                  == lax.broadcasted_iota(jnp.int32, (PAIR, HEAD_DIM), 1) + HEAD_DIM).astype(BF)

    def stack_heads(x):
        return jnp.concatenate([x * on_lo, x * on_hi], axis=0)

    def combine(y):
        half = y.shape[0] // 2
        n = y.shape[1] // PAIR
        m_lo = jnp.concatenate([on_lo] * n, axis=1)
        m_hi = jnp.concatenate([on_hi] * n, axis=1)
        return y[:half] * m_lo + y[half:] * m_hi

    def to_stacked(m):
        m_bf = m.astype(BF)
        bot = jnp.dot(m_bf, sel_hi, preferred_element_type=F32).astype(BF)
        return jnp.concatenate([m_bf[:, :HEAD_DIM], bot], axis=0)

    def run_solve(ps, tiles):
        n = len(tiles)
        span = 1
        ps = ps.astype(BF)
        while True:
            x = jnp.concatenate(tiles, axis=1)
            x_hi = x.astype(BF)
            x_lo = (x - x_hi.astype(F32)).astype(BF)
            y = jnp.dot(ps, jnp.concatenate([x_hi, x_lo], axis=1), preferred_element_type=F32)
            span *= 2
            if span < C:
                ps_next = jnp.concatenate(
                    [jnp.dot(ps[:C], ps[:C], preferred_element_type=F32),
                     jnp.dot(ps[C:], ps[C:], preferred_element_type=F32)], axis=0).astype(BF)
            yield
            pr = combine(y[:, :n * PAIR] + y[:, n * PAIR:])
            tiles = [t + pr[:, i * PAIR:(i + 1) * PAIR] for i, t in enumerate(tiles)]
            if span >= C:
                return tiles
            ps = ps_next


    def gdn_mixer(g):
        xc_ref[g, CARRY_ROWS:CARRY_ROWS + C, :] = load_rows(p_ref, g, OFF_QKV, 3 * A_WIDTH)
        cw = conv_w[...]
        y = xc_ref[g, c0:c0 + C, :] * cw[0:1]
        for j in range(1, CONV_W):
            y = y + xc_ref[g, c0 + j:c0 + j + C, :] * cw[j:j + 1]
        new_buf = xc_ref[g, c0 + t_valid:c0 + t_valid + keep, :]
        conv_s[g] = new_buf
        xc_ref[g, c0:CARRY_ROWS, :] = new_buf
        qkv = _silu(y)
        q_all = qkv[:, 0:A_WIDTH]
        k_all = qkv[:, A_WIDTH:2 * A_WIDTH]
        v_all = mask_rows(qkv[:, 2 * A_WIDTH:3 * A_WIDTH])
        z_all = load_rows(p_ref, g, OFF_Z, A_WIDTH)
        ba = load_rows(p_ref, g, OFF_BA, LANES)
        beta_all = mask_rows(_sigmoid(ba))
        g_all = mask_rows(-jnp.exp(alog[...]) * _softplus(ba + dtb[...]))
        gcum = cumsum_rows(g_all, _split3)
        q_ss = _head_sums(q_all * q_all)
        k_ss = _head_sums(k_all * k_all)
        yield
        src_lane = lax.broadcasted_iota(jnp.int32, (LANES, A_WIDTH), 0)
        dst_head = lax.broadcasted_iota(jnp.int32, (LANES, A_WIDTH), 1) // HEAD_DIM
        beta_x = _mm_sel_r(beta_all, jnp.where(src_lane == dst_head + GATE_LANE_BETA, 1.0, 0.0))
        gcum_x = _mm_sel_r(gcum, jnp.where(src_lane == dst_head + GATE_LANE_A, 1.0, 0.0))
        q_all = q_all * lax.rsqrt(q_ss + 1e-6) * HEAD_DIM ** -0.5
        k_all = mask_rows(k_all * lax.rsqrt(k_ss + 1e-6))
        yield
        if not single:
            r6 = lax.broadcasted_iota(jnp.int32, (C, A_WIDTH), 0)
            c6 = lax.broadcasted_iota(jnp.int32, (C, A_WIDTH), 1) & (HEAD_DIM - 1)
            gr_x = _mm_sel(jnp.ones((C, C), F32), jnp.where(c6 == r6, gcum_x, 0.0))
            yield
            decay_x = jnp.exp(jnp.minimum(gcum_x - gr_x, 0.0)) * as01(c6 <= r6)
        egc_x = jnp.exp(gcum_x)
        gl_x = gcum_x[C - 1:C, :]
        kdec_x = k_all * jnp.exp(gl_x - gcum_x)
        u0_x = v_all * beta_x
        w0_x = k_all * (beta_x * egc_x)
        qe_x = q_all * egc_x
        o_tiles = [None] * (H_A // 2)

        def head_pair(j):
            k = pair(k_all, j)
            decay = causal2 if single else pair(decay_x, j)
            krb = rbd(k)
            if not single:
                kk = _mm_nt(k, krb)
            qk = _mm_nt(pair(q_all, j), krb)
            qs = _mm(pair(qe_x, j), gdn_bd[g, j])
            yield
            if single:
                u, w = pair(u0_x, j), pair(w0_x, j)
            else:
                neg_m = to_stacked(-(pair(beta_x, j) * kk * decay) * strict2)
                qkd = to_stacked(qk * decay)
                yield
                u, w = yield from run_solve(neg_m, [pair(u0_x, j), pair(w0_x, j)])
            ws = _mm(w, gdn_bd[g, j])
            yield
            v_new = u - ws
            if single:
                o2 = _mm(qk * decay, rbd(v_new))
            else:
                o2 = combine(jnp.dot(qkd, v_new.astype(BF), preferred_element_type=F32))
            ds = _mm_tn(pair(kdec_x, j), v_new)
            yield
            o_tiles[j] = qs + o2
            gdn_bd[g, j] = gdn_bd[g, j] * jnp.exp(pair(gl_x, j)) + ds * bd_mask

        yield from _lockstep([head_pair(j) for j in range(H_A // 2)])
        o = jnp.concatenate(o_tiles, axis=1)
        o_ss = _head_sums(o * o)
        yield
        o = o * lax.rsqrt(o_ss * (1.0 / HEAD_DIM) + RMS_EPS) * gdn_nw[...] * _silu(z_all)
        store_rows(mix_ref, g, 0, o)

    def rwkv_mixer(g):
        pb = load_rows(p_ref, g, OFF_B, B_COLS)
        pbx_ref[g, CARRY_ROWS:CARRY_ROWS + C, :] = pb
        prev = pbx_ref[g, s0:s0 + C, :]
        new_shift = pbx_ref[g, s0 + t_valid:s0 + t_valid + 1, :]
        shift_s[g] = new_shift
        pbx_ref[g, s0:CARRY_ROWS, :] = new_shift
        xs = pb + (prev - pb) * mu[...]
        r_all = xs[:, 0:B_WIDTH]
        k_all = xs[:, B_WIDTH:2 * B_WIDTH]
        v_all = xs[:, 2 * B_WIDTH:3 * B_WIDTH]
        o_w = 3 * B_WIDTH
        wd = xs[:, o_w:o_w + W_LORA]
        ad = xs[:, o_w + W_LORA:o_w + W_LORA + A_LORA]
        gd = xs[:, o_w + W_LORA + A_LORA:o_w + W_LORA + A_LORA + G_LORA]
        w_lora = _mm(jnp.tanh(wd), w2[...])
        a_lora = _mm(ad, a2[...])
        g_out = _mm(_sigmoid(gd), g2[...])
        kkx = k_all * kk_w[...]
        kk_ss = _head_sums(kkx * kkx)
        if layer == 0:
            store_rows(vf_out_ref, g, 0, v_all)
        else:
            v_lora = _mm(v_all, v1[...])
        yield
        w_log = -_softplus(-(w0[...] + w_lora)) - 0.5
        log_decay = mask_rows(-jnp.exp(w_log))
        cum = cumsum_rows(log_decay)
        a_all = _sigmoid(a0[...] + a_lora)
        k_all = mask_rows(k_all * (1.0 + (a_all - 1.0) * ka_w[...]))
        rk_ss = _head_sums(r_all * k_all * rk_w[...])
        if layer > 0:
            v_mix = _mm(v_lora, v2[...])
        yield
        if layer > 0:
            mixv = _sigmoid(v0[...] + v_mix)
            v_all = v_all + (load_rows(vf_in_ref, g, 0, B_WIDTH) - v_all) * mixv
        v_all = mask_rows(v_all)
        kk = mask_rows(kkx * lax.rsqrt(kk_ss + 1e-6))
        bh = kk * a_all
        cl = cum[C - 1:C, :]
        einv = jnp.exp(-cum)
        ehat = jnp.exp(cl - cum)
        ar_x = jnp.concatenate([-kk * jnp.exp(cum - log_decay), r_all * jnp.exp(cum)], axis=0)
        bt_x = bh * einv
        kt_x = k_all * einv
        bk_x = jnp.concatenate([bh * ehat, k_all * ehat], axis=0)
        ecl_x = jnp.exp(cl)
        y_tiles = [None] * (H_B // 2)

        def head_pair(j):
            ar = pair(ar_x, j)
            v = pair(v_all, j)
            ars = _mm_nt(ar, rwkv_bd[g, j])
            gb = _mm_nt(ar, rbd(pair(bt_x, j)))
            gk = _mm_nt(ar, rbd(pair(kt_x, j)))
            yield
            gbm = gb * mask_ar2
            gkm = gk * mask_ar2
            lmv = _mm(gkm, rbd(v))
            if single:
                yield
                u = ars[:C]
            else:
                ps = to_stacked(gbm[:C])
                yield
                (u,) = yield from run_solve(ps, [ars[:C] + lmv[:C]])
            mrbu = _mm(gbm[C:], rbd(u))
            mrkv = lmv[C:]
            ds = _mm_tn(jnp.concatenate([u, v], axis=0), pair(bk_x, j))
            yield
            y_tiles[j] = ars[C:] + mrbu + mrkv
            rwkv_bd[g, j] = rwkv_bd[g, j] * pair(ecl_x, j) + ds * bd_mask

        yield from _lockstep([head_pair(j) for j in range(H_B // 2)])
        yh = jnp.concatenate(y_tiles, axis=1)
        mean = _head_sums(yh) * (1.0 / HEAD_DIM)
        yield
        yc = yh - mean
        var = _head_sums(yc * yc) * (1.0 / HEAD_DIM)
        yield
        yn = yc * lax.rsqrt(var + RWKV_LN_EPS) * lnw[...] + lnb[...]
        out = (yn + rk_ss * v_all) * g_out
        store_rows(mix_ref, g, A_WIDTH, out)

    def hgrn_mixer(g):
        pc = load_rows(p_ref, g, OFF_C, C_COLS)
        q_all = _silu(pc[:, 0:C_WIDTH]) * HEAD_DIM ** -0.5
        f_all = pc[:, C_WIDTH:2 * C_WIDTH]
        i_all = pc[:, 2 * C_WIDTH:3 * C_WIDTH]
        go_all = pc[:, 3 * C_WIDTH:4 * C_WIDTH]
        hl = hlb[...]
        hmax = jnp.max(hl, axis=0, keepdims=True)
        he = jnp.exp(hl - hmax)
        sm = he / jnp.sum(he, axis=0, keepdims=True)
        lb_acc = sm[0:1]
        for mrow in range(1, layer + 1):
            lb_acc = lb_acc + sm[mrow:mrow + 1]
        lb = lb_acc - sm[0:1]
        k_all = mask_rows((1.0 - lb) * _sigmoid(-f_all))
        log_f = jnp.log1p(-k_all)
        lsel = []
        pair_masks = []
        levels = []
        if not single:
            ls = 0
            while (1 << ls) < C:
                levels.append(ls)
                ls += 1
        for ls in levels:
            blk = row >> ls
            start = blk << ls
            odd = (blk & 1) == 1
            sel_q = odd & (col >= start) & (col <= row)
            sel_k = (~odd) & (col > row) & (col <= start + ((1 << ls) - 1))
            lsel.append(jnp.where(sel_q | sel_k, 1.0, 0.0).astype(F32))
            blk2 = row2 >> ls
            pair_masks.append(jnp.concatenate([as01(odd & ((col >> ls) == blk - 1))] * 2, axis=0))
        bcum = cumsum_rows(log_f)
        if levels:
            lvl_sums = _mm_sel(jnp.concatenate(lsel, axis=0), log_f)
        col_sums = _colsum_bcast(log_f)
        qk_ss = _head_sums(q_all * k_all)
        yield
        if levels:
            e_lvls = jnp.exp(lvl_sums)
        qe_x = q_all * jnp.exp(bcum)
        k_hat = k_all * jnp.exp(bcum[C - 1:C, :] - bcum)
        e_col = jnp.exp(col_sums)
        o_tiles = [None] * (H_C // 2)

        def head_pair(j):
            q = pair(q_all, j)
            k = pair(k_all, j)
            v = pair(i_all, j)
            parts = []
            for n in range(len(levels)):
                e = pair(e_lvls[n * C:(n + 1) * C], j)
                parts.append(_mm_nt(stack_heads(q * e), k * e))
            qs = _mm(pair(qe_x, j), hgrn_bd[g, j])
            ds = _mm_tn(pair(k_hat, j), v)
            yield
            o2 = pair(qk_ss, j) * v
            if levels:
                att = pair_masks[0] * parts[0]
                for n in range(1, len(levels)):
                    att = att + pair_masks[n] * parts[n]
                o2 = o2 + combine(_mm(att, v))
            yield
            o_tiles[j] = qs + o2
            hgrn_bd[g, j] = (e_col[j * PAIR:(j + 1) * PAIR, :] * hgrn_bd[g, j]
                             + ds * bd_mask)

        yield from _lockstep([head_pair(j) for j in range(H_C // 2)])
        o = jnp.concatenate(o_tiles, axis=1)
        o_ss = _head_sums(o * o)
        yield
        o = o * lax.rsqrt(o_ss * (1.0 / HEAD_DIM) + RMS_EPS) * hgrn_nw[...] * _silu(go_all)
        store_rows(mix_ref, g, A_WIDTH + B_WIDTH, o)

    def delayed(gen, rounds):
        for _ in range(rounds):
            yield
        yield from gen

    mixers = []
    for g in range(group):
        mixers += [delayed(m, g * GROUP_STAGGER) for m in (gdn_mixer(g), rwkv_mixer(g), hgrn_mixer(g))]
    for _ in _lockstep(mixers):
        pass

    if single:
        store_states()
    else:
        pl.when(t == last_t)(store_states)


def _mixers(layer, p, vfirst, states_in, state_layer, acc, lp, rows, single, group):
    bsz, t, _ = p.shape
    t_blk = 1 if single else rows
    nt = t // t_blk
    assert nt * t_blk == t and bsz % group == 0 and (not single or t == 1)

    def per_bt(width):
        return pl.BlockSpec((group, t_blk, width), lambda b, t_: (b, t_, 0))

    def per_b(shape, lyr):
        nd = len(shape)
        return pl.BlockSpec((None, group) + shape, lambda b, t_: (lyr, b) + (0,) * nd)

    def const(arr):
        nd = arr.ndim
        return pl.BlockSpec(arr.shape, lambda b, t_: (0,) * nd)

    inputs = [p]
    in_specs = [per_bt(P_COLS)]
    if layer > 0:
        inputs.append(vfirst)
        in_specs.append(per_bt(B_WIDTH))
    inputs += list(states_in)
    in_specs += [per_b(s, state_layer) for s in STATE_SHAPES]
    names = ["conv_w", "alog", "dtb", "gdn_nw", "mu", "w0", "w2", "a0", "a2", "g2"]
    if layer > 0:
        names += ["v0", "v1", "v2"]
    names += ["kk_w", "ka_w", "rk_w", "lnw", "lnb", "hlb", "hgrn_nw"]
    for nm in names:
        inputs.append(lp[nm])
        in_specs.append(const(lp[nm]))
    acc_first = len(inputs)
    inputs += list(acc)
    in_specs += [pl.BlockSpec(memory_space=pl.ANY) for _ in acc]

    out_shape = [jax.ShapeDtypeStruct((bsz, t, D_MODEL), BF)]
    out_specs = [per_bt(D_MODEL)]
    if layer == 0:
        out_shape.append(jax.ShapeDtypeStruct((bsz, t, B_WIDTH), F32))
        out_specs.append(per_bt(B_WIDTH))
    state_first = len(out_shape)
    for s, a in zip(STATE_SHAPES, acc):
        out_shape.append(jax.ShapeDtypeStruct(a.shape, F32))
        out_specs.append(per_b(s, layer))
    aliases = {acc_first + i: state_first + i for i in range(len(acc))}

    outs = pl.pallas_call(
        functools.partial(_mixer_kernel, layer, rows, single, group),
        grid=(bsz // group, nt),
        in_specs=in_specs,
        out_specs=out_specs,
        out_shape=out_shape,
        input_output_aliases=aliases,
        scratch_shapes=[pltpu.VMEM((group, CARRY_ROWS + rows, 3 * A_WIDTH), F32),
                        pltpu.VMEM((group, CARRY_ROWS + rows, B_COLS), F32),
                        pltpu.VMEM((group, H_A // 2, PAIR, PAIR), F32),
                        pltpu.VMEM((group, H_B // 2, PAIR, PAIR), F32),
                        pltpu.VMEM((group, H_C // 2, PAIR, PAIR), F32)],
        compiler_params=pltpu.CompilerParams(dimension_semantics=("arbitrary", "arbitrary")),
        name=f"mixers_l{layer}_r{rows}",
    )(*inputs)
    mix = outs[0]
    if layer == 0:
        vfirst = outs[1]
    return mix, vfirst, tuple(outs[state_first:])


def _trunk(x, states_in, layered_states, W):
    bsz, t, _ = x.shape
    if t == 1:
        rows, single, group = DECODE_ROWS, True, DECODE_GROUP
    else:
        assert t % PROMPT_CHUNK == 0
        rows, single, group = PROMPT_CHUNK, False, PROMPT_GROUP
    n = bsz * t
    xf = x.reshape(n, D_MODEL)
    fill = jnp.minimum(jnp.abs(xf[0, 0]), 0.0)
    acc = tuple(jnp.full((DEPTH, bsz) + s, fill, F32) for s in STATE_SHAPES)
    vfirst = None
    for l in range(DEPTH):
        lp = W["layers"][l]
        p = _inproj(xf, lp["norm_mix"], W["w_in"], l).reshape(bsz, t, P_COLS)
        mix, vfirst, acc = _mixers(l, p, vfirst, states_in, l if layered_states else 0, acc, lp,
                                   rows, single, group)
        xf = _ffn(xf, mix.reshape(n, D_MODEL), W["w_out"], lp["norm_ffn"], W["w_gate"], W["w_up"],
                  W["w_down"], W["norm_final"], l, final=(l == DEPTH - 1))
    y = xf.reshape(bsz, t, D_MODEL)
    gdn, conv, rwkv, shift, hgrn = acc
    return y, (gdn, conv, rwkv, shift[:, :, 0, :], hgrn)


def _prepare_weights(norm_mix, w_in, w_out, gdn_conv_w, gdn_a_log, gdn_dt_bias, gdn_norm_w,
                     rwkv_mu, rwkv_w0, rwkv_w2, rwkv_a0, rwkv_a2, rwkv_g2, rwkv_v0, rwkv_v1, rwkv_v2,
                     rwkv_k_k, rwkv_k_a, rwkv_r_k, rwkv_ln_w, rwkv_ln_b,
                     hgrn_lower_bounds, hgrn_norm_w, norm_ffn, w_gate, w_up, w_down, norm_final):
    depth = w_in.shape[0]
    ba_cols = w_in[:, :, 4 * A_WIDTH:A_COLS]
    pad = jnp.zeros((depth, D_MODEL, LANES - 2 * H_A), w_in.dtype)
    w_in_r = jnp.concatenate([w_in[:, :, :4 * A_WIDTH], w_in[:, :, A_COLS:], ba_cols, pad], axis=-1).astype(BF)

    def gate_row(v):
        z_lo = jnp.zeros((depth, GATE_LANE_A), F32)
        z_hi = jnp.zeros((depth, LANES - GATE_LANE_A - H_A), F32)
        return jnp.concatenate([z_lo, v.astype(F32), z_hi], axis=-1)[:, None, :]

    alog = gate_row(gdn_a_log)
    dtb = gate_row(gdn_dt_bias)
    layers = []
    for l in range(depth):
        lp = {
            "norm_mix": norm_mix[l][None], "norm_ffn": norm_ffn[l][None],
            "conv_w": gdn_conv_w[l], "alog": alog[l], "dtb": dtb[l],
            "gdn_nw": jnp.tile(gdn_norm_w[l], H_A)[None],
            "mu": rwkv_mu[l][None], "w0": rwkv_w0[l][None], "w2": rwkv_w2[l], "a0": rwkv_a0[l][None],
            "a2": rwkv_a2[l], "g2": rwkv_g2[l],
            "kk_w": rwkv_k_k[l][None], "ka_w": rwkv_k_a[l][None], "rk_w": rwkv_r_k[l].reshape(1, B_WIDTH),
            "lnw": rwkv_ln_w[l][None], "lnb": rwkv_ln_b[l][None],
            "hlb": hgrn_lower_bounds, "hgrn_nw": jnp.tile(hgrn_norm_w[l], H_C)[None],
        }
        if l > 0:
            lp.update(v0=rwkv_v0[l - 1][None], v1=rwkv_v1[l - 1], v2=rwkv_v2[l - 1])
        layers.append(lp)
    return {"layers": layers, "norm_final": norm_final[None], "w_in": w_in_r, "w_out": w_out.astype(BF),
            "w_gate": w_gate.astype(BF), "w_up": w_up.astype(BF), "w_down": w_down.astype(BF)}


def kernel(x_prompt, x_sample, state_gdn, state_gdn_conv, state_rwkv, state_rwkv_shift, state_hgrn, norm_mix, w_in, w_out, gdn_conv_w, gdn_a_log, gdn_dt_bias, gdn_norm_w, rwkv_mu, rwkv_w0, rwkv_w2, rwkv_a0, rwkv_a2, rwkv_g2, rwkv_v0, rwkv_v1, rwkv_v2, rwkv_k_k, rwkv_k_a, rwkv_r_k, rwkv_ln_w, rwkv_ln_b, hgrn_lower_bounds, hgrn_norm_w, norm_ffn, w_gate, w_up, w_down, norm_final):
    W = _prepare_weights(norm_mix, w_in, w_out, gdn_conv_w, gdn_a_log, gdn_dt_bias, gdn_norm_w,
                         rwkv_mu, rwkv_w0, rwkv_w2, rwkv_a0, rwkv_a2, rwkv_g2, rwkv_v0, rwkv_v1, rwkv_v2,
                         rwkv_k_k, rwkv_k_a, rwkv_r_k, rwkv_ln_w, rwkv_ln_b,
                         hgrn_lower_bounds, hgrn_norm_w, norm_ffn, w_gate, w_up, w_down, norm_final)
    b = x_prompt.shape[0]
    zero_states = tuple(jnp.zeros((1, b) + s, F32) for s in STATE_SHAPES)
    y_prompt, (gdn_p, conv_p, rwkv_p, shift_p, hgrn_p) = _trunk(x_prompt, zero_states, False, W)
    sample_states = (state_gdn, state_gdn_conv, state_rwkv, state_rwkv_shift[:, :, None, :], state_hgrn)
    y_sample, (gdn_s, conv_s, rwkv_s, shift_s, hgrn_s) = _trunk(x_sample, sample_states, True, W)
    return (y_prompt, y_sample, gdn_p, gdn_s, conv_p, conv_s, rwkv_p, rwkv_s, shift_p, shift_s, hgrn_p, hgrn_s)
```

```python
import functools

import jax
import jax.numpy as jnp
from jax import lax
from jax.experimental import pallas as pl
from jax.experimental.pallas import tpu as pltpu

D_MODEL = 1024
DEPTH = 4
HEAD_DIM = 64
A_WIDTH = 384
B_WIDTH = 384
C_WIDTH = 256
H_A = A_WIDTH // HEAD_DIM
H_B = B_WIDTH // HEAD_DIM
H_C = C_WIDTH // HEAD_DIM
CONV_W = 4
W_LORA = 64
A_LORA = 64
G_LORA = 128
V_LORA = 32
A_COLS = 4 * A_WIDTH + 2 * H_A
B_COLS = 3 * B_WIDTH + W_LORA + A_LORA + G_LORA
C_COLS = 4 * C_WIDTH
D_FF = 2816
RMS_EPS = 1e-6
RWKV_LN_EPS = 64e-5

LANES = 128
PAIR = 2 * HEAD_DIM
OFF_QKV = 0
OFF_Z = 3 * A_WIDTH
OFF_B = 4 * A_WIDTH
OFF_C = OFF_B + B_COLS
OFF_BA = OFF_C + C_COLS
P_COLS = OFF_BA + LANES
GATE_LANE_BETA = 0
GATE_LANE_A = H_A

PROMPT_CHUNK = 64
PROMPT_GROUP = 2
GROUP_STAGGER = 0
DECODE_ROWS = 8
DECODE_GROUP = 4
CARRY_ROWS = 8

STATE_SHAPES = ((H_A, HEAD_DIM, HEAD_DIM), (CONV_W - 1, 3 * A_WIDTH), (H_B, HEAD_DIM, HEAD_DIM),
                (1, B_COLS), (H_C, HEAD_DIM, HEAD_DIM))

BF = jnp.bfloat16
F32 = jnp.float32


def _mm(a, b):
    return jnp.dot(a.astype(BF), b.astype(BF), preferred_element_type=F32)


def _mm_nt(a, b):
    return lax.dot_general(a.astype(BF), b.astype(BF), (((1,), (1,)), ((), ())),
                           preferred_element_type=F32)


def _mm_tn(a, b):
    return lax.dot_general(a.astype(BF), b.astype(BF), (((0,), (0,)), ((), ())),
                           preferred_element_type=F32)


def _split3(x):
    x1 = x.astype(BF)
    r1 = x - x1.astype(F32)
    x2 = r1.astype(BF)
    x3 = (r1 - x2.astype(F32)).astype(BF)
    return x1, x2, x3


def _split2(x):
    x1 = x.astype(BF)
    x2 = (x - x1.astype(F32)).astype(BF)
    return x1, x2


def _mm_sel(sel, x, parts=_split2):
    s = sel.astype(BF)
    out = None
    for xp in parts(x):
        d = jnp.dot(s, xp, preferred_element_type=F32)
        out = d if out is None else out + d
    return out


def _mm_sel_r(x, sel, parts=_split2):
    s = sel.astype(BF)
    out = None
    for xp in parts(x):
        d = jnp.dot(xp, s, preferred_element_type=F32)
        out = d if out is None else out + d
    return out


def _colsum_bcast(x):
    ones = jnp.ones((x.shape[0], LANES), BF)
    dn = (((0,), (0,)), ((), ()))
    out = None
    for xp in _split2(x):
        d = lax.dot_general(xp, ones, dn, preferred_element_type=F32)
        out = d if out is None else out + d
    return out


def _head_sums(x):
    on_lo = jnp.where(lax.broadcasted_iota(jnp.int32, (1, PAIR), 1) < HEAD_DIM, 1.0, 0.0)
    on_hi = 1.0 - on_lo
    tiles = []
    for j in range(x.shape[-1] // PAIR):
        t = x[:, j * PAIR:(j + 1) * PAIR]
        s_lo = jnp.sum(t * on_lo, -1, keepdims=True)
        s_hi = jnp.sum(t * on_hi, -1, keepdims=True)
        tiles.append(s_lo * on_lo + s_hi * on_hi)
    return jnp.concatenate(tiles, axis=1)


def _sigmoid(x):
    return 1.0 / (1.0 + jnp.exp(-x))


def _silu(x):
    return x * _sigmoid(x)


def _softplus(x):
    return jnp.maximum(x, 0.0) + jnp.log(1.0 + jnp.exp(-jnp.abs(x)))


def _rmsnorm(x, w):
    return x * lax.rsqrt(jnp.mean(x * x, -1, keepdims=True) + RMS_EPS) * w


def _lockstep(gens):
    gens = list(gens)
    while gens:
        alive = []
        for g in gens:
            try:
                next(g)
                alive.append(g)
            except StopIteration:
                pass
        gens = alive
        if gens:
            yield


def _inproj_kernel(x_ref, nw_ref, w_ref, o_ref):
    h = _rmsnorm(x_ref[...], nw_ref[...]).astype(BF)
    step = 512
    for c in range(0, P_COLS, step):
        o_ref[:, c:c + step] = jnp.dot(h, w_ref[:, c:c + step], preferred_element_type=F32)


def _inproj(x, norm_w, w, layer):
    n = x.shape[0]
    tm = min(512, n)
    assert n % tm == 0
    return pl.pallas_call(
        _inproj_kernel,
        grid=(n // tm,),
        in_specs=[
            pl.BlockSpec((tm, D_MODEL), lambda i: (i, 0)),
            pl.BlockSpec((1, D_MODEL), lambda i: (0, 0)),
            pl.BlockSpec((None, D_MODEL, P_COLS), lambda i: (layer, 0, 0)),
        ],
        out_specs=pl.BlockSpec((tm, P_COLS), lambda i: (i, 0)),
        out_shape=jax.ShapeDtypeStruct((n, P_COLS), F32),
        compiler_params=pltpu.CompilerParams(dimension_semantics=("arbitrary",)),
        name="inproj",
    )(x, norm_w, w)


def _ffn_kernel(final, x_ref, mix_ref, wo_ref, nf_ref, wg_ref, wu_ref, wd_ref, nfin_ref, o_ref):
    x = x_ref[...] + jnp.dot(mix_ref[...], wo_ref[...], preferred_element_type=F32)
    h = _rmsnorm(x, nf_ref[...]).astype(BF)
    half = D_FF // 2
    acc = x
    for c in range(0, D_FF, half):
        gate = jnp.dot(h, wg_ref[:, c:c + half], preferred_element_type=F32)
        up = jnp.dot(h, wu_ref[:, c:c + half], preferred_element_type=F32)
        u = (_silu(gate) * up).astype(BF)
        acc = acc + jnp.dot(u, wd_ref[c:c + half, :], preferred_element_type=F32)
    if final:
        acc = _rmsnorm(acc, nfin_ref[...])
    o_ref[...] = acc


def _ffn(x, mix, w_out, norm_ffn, w_gate, w_up, w_down, norm_final, layer, final):
    n = x.shape[0]
    tm = min(512, n)
    assert n % tm == 0
    const = lambda i: (0, 0)
    of_layer = lambda i: (layer, 0, 0)
    once = pl.Buffered(1)
    return pl.pallas_call(
        functools.partial(_ffn_kernel, final),
        grid=(n // tm,),
        in_specs=[
            pl.BlockSpec((tm, D_MODEL), lambda i: (i, 0)),
            pl.BlockSpec((tm, D_MODEL), lambda i: (i, 0)),
            pl.BlockSpec((None, D_MODEL, D_MODEL), of_layer, pipeline_mode=once),
            pl.BlockSpec((1, D_MODEL), const),
            pl.BlockSpec((None, D_MODEL, D_FF), of_layer, pipeline_mode=once),
            pl.BlockSpec((None, D_MODEL, D_FF), of_layer, pipeline_mode=once),
            pl.BlockSpec((None, D_FF, D_MODEL), of_layer, pipeline_mode=once),
            pl.BlockSpec((1, D_MODEL), const),
        ],
        out_specs=pl.BlockSpec((tm, D_MODEL), lambda i: (i, 0)),
        out_shape=jax.ShapeDtypeStruct((n, D_MODEL), F32),
        compiler_params=pltpu.CompilerParams(dimension_semantics=("arbitrary",),
                                             vmem_limit_bytes=56 * 1024 * 1024),
        name="ffn",
    )(x, mix, w_out, norm_ffn, w_gate, w_up, w_down, norm_final)


def _mixer_kernel(layer, rows, single, group, *refs):
    C = rows
    assert single or C == HEAD_DIM
    it = iter(refs)
    p_ref = next(it)
    vf_in_ref = next(it) if layer > 0 else None
    gdn_s0, conv0, rwkv_s0, shift0, hgrn_s0 = (next(it) for _ in range(5))
    conv_w, alog, dtb, gdn_nw = (next(it) for _ in range(4))
    mu, w0, w2, a0, a2, g2 = (next(it) for _ in range(6))
    if layer > 0:
        v0, v1, v2 = (next(it) for _ in range(3))
    kk_w, ka_w, rk_w, lnw, lnb = (next(it) for _ in range(5))
    hlb, hgrn_nw = (next(it) for _ in range(2))
    for _ in STATE_SHAPES:
        next(it)
    mix_ref = next(it)
    vf_out_ref = next(it) if layer == 0 else None
    gdn_s, conv_s, rwkv_s, shift_s, hgrn_s = (next(it) for _ in range(5))
    xc_ref, pbx_ref, gdn_bd, rwkv_bd, hgrn_bd = (next(it) for _ in range(5))

    t = pl.program_id(1)
    last_t = pl.num_programs(1) - 1
    keep = CONV_W - 1
    c0 = CARRY_ROWS - keep
    s0 = CARRY_ROWS - 1
    t_valid = 1 if single else C
    lo, hi = slice(0, HEAD_DIM), slice(HEAD_DIM, PAIR)
    state_io = ((gdn_s0, gdn_s, gdn_bd, H_A), (rwkv_s0, rwkv_s, rwkv_bd, H_B), (hgrn_s0, hgrn_s, hgrn_bd, H_C))

    def load_states():
        for src, _, bd, n_heads in state_io:
            bd[...] = jnp.zeros(bd.shape, F32)
            for g in range(group):
                for j in range(n_heads // 2):
                    bd[g, j, lo, lo] = src[g, 2 * j]
                    bd[g, j, hi, hi] = src[g, 2 * j + 1]
        xc_ref[:, c0:CARRY_ROWS, :] = conv0[...]
        pbx_ref[:, s0:CARRY_ROWS, :] = shift0[...]

    def store_states():
        for _, dst, bd, n_heads in state_io:
            for g in range(group):
                for j in range(n_heads // 2):
                    dst[g, 2 * j] = bd[g, j, lo, lo]
                    dst[g, 2 * j + 1] = bd[g, j, hi, hi]

    if single:
        load_states()
    else:
        pl.when(t == 0)(load_states)

    row = lax.broadcasted_iota(jnp.int32, (C, C), 0)
    col = lax.broadcasted_iota(jnp.int32, (C, C), 1)
    tri = jnp.where(col <= row, 1.0, 0.0).astype(F32)
    row2 = lax.broadcasted_iota(jnp.int32, (C, 2 * C), 0)
    col2 = lax.broadcasted_iota(jnp.int32, (C, 2 * C), 1) & (C - 1)
    def as01(cond):
        return jnp.where(cond, 1.0, 0.0).astype(F32)

    causal2 = as01(col2 <= row2)
    strict2 = as01(col2 < row2)
    mask_ar2 = jnp.concatenate([strict2, causal2], axis=0)
    lane_lo = lax.broadcasted_iota(jnp.int32, (1, PAIR), 1) < HEAD_DIM
    on_lo = as01(lane_lo)
    on_hi = as01(~lane_lo)
    bd_mask = as01((lax.broadcasted_iota(jnp.int32, (PAIR, PAIR), 0) < HEAD_DIM)
                   == (lax.broadcasted_iota(jnp.int32, (PAIR, PAIR), 1) < HEAD_DIM))
    if single:
        valid = lax.broadcasted_iota(jnp.int32, (C, 1), 0) < 1

    def mask_rows(x):
        return jnp.where(valid, x, 0.0) if single else x

    def load_rows(ref, g, start, width):
        x = ref[g, :, start:start + width]
        return jnp.broadcast_to(x, (C, width)) if single else x

    def store_rows(ref, g, start, val):
        val = val[0:1] if single else val
        ref[g, :, start:start + val.shape[-1]] = val.astype(ref.dtype)

    def cumsum_rows(x, parts=_split2):
        if single:
            return jnp.broadcast_to(x[0:1], x.shape)
        return _mm_sel(tri, x, parts)

    def rbd(x):
        return jnp.concatenate([x * on_lo, x * on_hi], axis=0).astype(BF)

    def pair_cols(x):
        if C == HEAD_DIM:
            return x
        return jnp.concatenate([x[:, :C], x[:, HEAD_DIM:HEAD_DIM + C]], axis=1)

    def pair(x, j):
        return x[:, j * PAIR:(j + 1) * PAIR]

    sel_hi = as01(lax.broadcasted_iota(jnp.int32, (PAIR, HEAD_DIM), 0)
                  == lax.broadcasted_iota(jnp.int32, (PAIR, HEAD_DIM), 1) + HEAD_DIM).astype(BF)

    def stack_heads(x):
        return jnp.concatenate([x * on_lo, x * on_hi], axis=0)

    def combine(y):
        half = y.shape[0] // 2
        n = y.shape[1] // PAIR
        m_lo = jnp.concatenate([on_lo] * n, axis=1)
        m_hi = jnp.concatenate([on_hi] * n, axis=1)
        return y[:half] * m_lo + y[half:] * m_hi

    def to_stacked(m):
        m_bf = m.astype(BF)
        bot = jnp.dot(m_bf, sel_hi, preferred_element_type=F32).astype(BF)
        return jnp.concatenate([m_bf[:, :HEAD_DIM], bot], axis=0)

    def run_solve(ps, tiles):
        n = len(tiles)
        span = 1
        ps = ps.astype(BF)
        while True:
            x = jnp.concatenate(tiles, axis=1)
            x_hi = x.astype(BF)
            x_lo = (x - x_hi.astype(F32)).astype(BF)
            y = jnp.dot(ps, jnp.concatenate([x_hi, x_lo], axis=1), preferred_element_type=F32)
            span *= 2
            if span < C:
                ps_next = jnp.concatenate(
                    [jnp.dot(ps[:C], ps[:C], preferred_element_type=F32),
                     jnp.dot(ps[C:], ps[C:], preferred_element_type=F32)], axis=0).astype(BF)
            yield
            pr = combine(y[:, :n * PAIR] + y[:, n * PAIR:])
            tiles = [t + pr[:, i * PAIR:(i + 1) * PAIR] for i, t in enumerate(tiles)]
            if span >= C:
                return tiles
            ps = ps_next


    def gdn_mixer(g):
        xc_ref[g, CARRY_ROWS:CARRY_ROWS + C, :] = load_rows(p_ref, g, OFF_QKV, 3 * A_WIDTH)
        cw = conv_w[...]
        y = xc_ref[g, c0:c0 + C, :] * cw[0:1]
        for j in range(1, CONV_W):
            y = y + xc_ref[g, c0 + j:c0 + j + C, :] * cw[j:j + 1]
        new_buf = xc_ref[g, c0 + t_valid:c0 + t_valid + keep, :]
        conv_s[g] = new_buf
        xc_ref[g, c0:CARRY_ROWS, :] = new_buf
        qkv = _silu(y)
        q_all = qkv[:, 0:A_WIDTH]
        k_all = qkv[:, A_WIDTH:2 * A_WIDTH]
        v_all = mask_rows(qkv[:, 2 * A_WIDTH:3 * A_WIDTH])
        z_all = load_rows(p_ref, g, OFF_Z, A_WIDTH)
        ba = load_rows(p_ref, g, OFF_BA, LANES)
        beta_all = mask_rows(_sigmoid(ba))
        g_all = mask_rows(-jnp.exp(alog[...]) * _softplus(ba + dtb[...]))
        gcum = cumsum_rows(g_all, _split3)
        q_ss = _head_sums(q_all * q_all)
        k_ss = _head_sums(k_all * k_all)
        yield
        src_lane = lax.broadcasted_iota(jnp.int32, (LANES, A_WIDTH), 0)
        dst_head = lax.broadcasted_iota(jnp.int32, (LANES, A_WIDTH), 1) // HEAD_DIM
        beta_x = _mm_sel_r(beta_all, jnp.where(src_lane == dst_head + GATE_LANE_BETA, 1.0, 0.0))
        gcum_x = _mm_sel_r(gcum, jnp.where(src_lane == dst_head + GATE_LANE_A, 1.0, 0.0))
        q_all = q_all * lax.rsqrt(q_ss + 1e-6) * HEAD_DIM ** -0.5
        k_all = mask_rows(k_all * lax.rsqrt(k_ss + 1e-6))
        yield
        if not single:
            r6 = lax.broadcasted_iota(jnp.int32, (C, A_WIDTH), 0)
            c6 = lax.broadcasted_iota(jnp.int32, (C, A_WIDTH), 1) & (HEAD_DIM - 1)
            gr_x = _mm_sel(jnp.ones((C, C), F32), jnp.where(c6 == r6, gcum_x, 0.0))
            yield
            decay_x = jnp.exp(jnp.minimum(gcum_x - gr_x, 0.0)) * as01(c6 <= r6)
        egc_x = jnp.exp(gcum_x)
        gl_x = gcum_x[C - 1:C, :]
        kdec_x = k_all * jnp.exp(gl_x - gcum_x)
        u0_x = v_all * beta_x
        w0_x = k_all * (beta_x * egc_x)
        qe_x = q_all * egc_x
        o_tiles = [None] * (H_A // 2)

        def head_pair(j):
            k = pair(k_all, j)
            decay = causal2 if single else pair(decay_x, j)
            krb = rbd(k)
            if not single:
                kk = _mm_nt(k, krb)
            qk = _mm_nt(pair(q_all, j), krb)
            qs = _mm(pair(qe_x, j), gdn_bd[g, j])
            yield
            if single:
                u, w = pair(u0_x, j), pair(w0_x, j)
            else:
                neg_m = to_stacked(-(pair(beta_x, j) * kk * decay) * strict2)
                qkd = to_stacked(qk * decay)
                yield
                u, w = yield from run_solve(neg_m, [pair(u0_x, j), pair(w0_x, j)])
            ws = _mm(w, gdn_bd[g, j])
            yield
            v_new = u - ws
            if single:
                o2 = _mm(qk * decay, rbd(v_new))
            else:
                o2 = combine(jnp.dot(qkd, v_new.astype(BF), preferred_element_type=F32))
            ds = _mm_tn(pair(kdec_x, j), v_new)
            yield
            o_tiles[j] = qs + o2
            gdn_bd[g, j] = gdn_bd[g, j] * jnp.exp(pair(gl_x, j)) + ds * bd_mask

        yield from _lockstep([head_pair(j) for j in range(H_A // 2)])
        o = jnp.concatenate(o_tiles, axis=1)
        o_ss = _head_sums(o * o)
        yield
        o = o * lax.rsqrt(o_ss * (1.0 / HEAD_DIM) + RMS_EPS) * gdn_nw[...] * _silu(z_all)
        store_rows(mix_ref, g, 0, o)

    def rwkv_mixer(g):
        pb = load_rows(p_ref, g, OFF_B, B_COLS)
        pbx_ref[g, CARRY_ROWS:CARRY_ROWS + C, :] = pb
        prev = pbx_ref[g, s0:s0 + C, :]
        new_shift = pbx_ref[g, s0 + t_valid:s0 + t_valid + 1, :]
        shift_s[g] = new_shift
        pbx_ref[g, s0:CARRY_ROWS, :] = new_shift
        xs = pb + (prev - pb) * mu[...]
        r_all = xs[:, 0:B_WIDTH]
        k_all = xs[:, B_WIDTH:2 * B_WIDTH]
        v_all = xs[:, 2 * B_WIDTH:3 * B_WIDTH]
        o_w = 3 * B_WIDTH
        wd = xs[:, o_w:o_w + W_LORA]
        ad = xs[:, o_w + W_LORA:o_w + W_LORA + A_LORA]
        gd = xs[:, o_w + W_LORA + A_LORA:o_w + W_LORA + A_LORA + G_LORA]
        w_lora = _mm(jnp.tanh(wd), w2[...])
        a_lora = _mm(ad, a2[...])
        g_out = _mm(_sigmoid(gd), g2[...])
        kkx = k_all * kk_w[...]
        kk_ss = _head_sums(kkx * kkx)
        if layer == 0:
            store_rows(vf_out_ref, g, 0, v_all)
        else:
            v_lora = _mm(v_all, v1[...])
        yield
        w_log = -_softplus(-(w0[...] + w_lora)) - 0.5
        log_decay = mask_rows(-jnp.exp(w_log))
        cum = cumsum_rows(log_decay)
        a_all = _sigmoid(a0[...] + a_lora)
        k_all = mask_rows(k_all * (1.0 + (a_all - 1.0) * ka_w[...]))
        rk_ss = _head_sums(r_all * k_all * rk_w[...])
        if layer > 0:
            v_mix = _mm(v_lora, v2[...])
        yield
        if layer > 0:
            mixv = _sigmoid(v0[...] + v_mix)
            v_all = v_all + (load_rows(vf_in_ref, g, 0, B_WIDTH) - v_all) * mixv
        v_all = mask_rows(v_all)
        kk = mask_rows(kkx * lax.rsqrt(kk_ss + 1e-6))
        bh = kk * a_all
        cl = cum[C - 1:C, :]
        einv = jnp.exp(-cum)
        ehat = jnp.exp(cl - cum)
        ar_x = jnp.concatenate([-kk * jnp.exp(cum - log_decay), r_all * jnp.exp(cum)], axis=0)
        bt_x = bh * einv
        kt_x = k_all * einv
        bk_x = jnp.concatenate([bh * ehat, k_all * ehat], axis=0)
        ecl_x = jnp.exp(cl)
        y_tiles = [None] * (H_B // 2)

        def head_pair(j):
            ar = pair(ar_x, j)
            v = pair(v_all, j)
            ars = _mm_nt(ar, rwkv_bd[g, j])
            gb = _mm_nt(ar, rbd(pair(bt_x, j)))
            gk = _mm_nt(ar, rbd(pair(kt_x, j)))
            yield
            gbm = gb * mask_ar2
            gkm = gk * mask_ar2
            lmv = _mm(gkm, rbd(v))
            if single:
                yield
                u = ars[:C]
            else:
                ps = to_stacked(gbm[:C])
                yield
                (u,) = yield from run_solve(ps, [ars[:C] + lmv[:C]])
            mrbu = _mm(gbm[C:], rbd(u))
            mrkv = lmv[C:]
            ds = _mm_tn(jnp.concatenate([u, v], axis=0), pair(bk_x, j))
            yield
            y_tiles[j] = ars[C:] + mrbu + mrkv
            rwkv_bd[g, j] = rwkv_bd[g, j] * pair(ecl_x, j) + ds * bd_mask

        yield from _lockstep([head_pair(j) for j in range(H_B // 2)])
        yh = jnp.concatenate(y_tiles, axis=1)
        mean = _head_sums(yh) * (1.0 / HEAD_DIM)
        yield
        yc = yh - mean
        var = _head_sums(yc * yc) * (1.0 / HEAD_DIM)
        yield
        yn = yc * lax.rsqrt(var + RWKV_LN_EPS) * lnw[...] + lnb[...]
        out = (yn + rk_ss * v_all) * g_out
        store_rows(mix_ref, g, A_WIDTH, out)

    def hgrn_mixer(g):
        pc = load_rows(p_ref, g, OFF_C, C_COLS)
        q_all = _silu(pc[:, 0:C_WIDTH]) * HEAD_DIM ** -0.5
        f_all = pc[:, C_WIDTH:2 * C_WIDTH]
        i_all = pc[:, 2 * C_WIDTH:3 * C_WIDTH]
        go_all = pc[:, 3 * C_WIDTH:4 * C_WIDTH]
        hl = hlb[...]
        hmax = jnp.max(hl, axis=0, keepdims=True)
        he = jnp.exp(hl - hmax)
        sm = he / jnp.sum(he, axis=0, keepdims=True)
        lb_acc = sm[0:1]
        for mrow in range(1, layer + 1):
            lb_acc = lb_acc + sm[mrow:mrow + 1]
        lb = lb_acc - sm[0:1]
        k_all = mask_rows((1.0 - lb) * _sigmoid(-f_all))
        log_f = jnp.log(1.0 - k_all)
        lsel = []
        pair_masks = []
        levels = []
        if not single:
            ls = 0
            while (1 << ls) < C:
                levels.append(ls)
                ls += 1
        for ls in levels:
            blk = row >> ls
            start = blk << ls
            odd = (blk & 1) == 1
            sel_q = odd & (col >= start) & (col <= row)
            sel_k = (~odd) & (col > row) & (col <= start + ((1 << ls) - 1))
            lsel.append(jnp.where(sel_q | sel_k, 1.0, 0.0).astype(F32))
            pair_masks.append(jnp.concatenate([as01(odd & ((col >> ls) == blk - 1))] * 2, axis=0))
        bcum = cumsum_rows(log_f)
        if levels:
            lvl_sums = _mm_sel(jnp.concatenate(lsel, axis=0), log_f)
        col_sums = _colsum_bcast(log_f)
        qk_ss = _head_sums(q_all * k_all)
        yield
        if levels:
            e_lvls = jnp.exp(lvl_sums)
        qe_x = q_all * jnp.exp(bcum)
        k_hat = k_all * jnp.exp(bcum[C - 1:C, :] - bcum)
        e_col = jnp.exp(col_sums)
        o_tiles = [None] * (H_C // 2)

        def head_pair(j):
            q = pair(q_all, j)
            k = pair(k_all, j)
            v = pair(i_all, j)
            parts = []
            for n in range(len(levels)):
                e = pair(e_lvls[n * C:(n + 1) * C], j)
                parts.append(_mm_nt(stack_heads(q * e), k * e))
            qs = _mm(pair(qe_x, j), hgrn_bd[g, j])
            ds = _mm_tn(pair(k_hat, j), v)
            yield
            o2 = pair(qk_ss, j) * v
            if levels:
                att = pair_masks[0] * parts[0]
                for n in range(1, len(levels)):
                    att = att + pair_masks[n] * parts[n]
                o2 = o2 + combine(_mm(att, v))
            yield
            o_tiles[j] = qs + o2
            hgrn_bd[g, j] = (e_col[j * PAIR:(j + 1) * PAIR, :] * hgrn_bd[g, j]
                             + ds * bd_mask)

        yield from _lockstep([head_pair(j) for j in range(H_C // 2)])
        o = jnp.concatenate(o_tiles, axis=1)
        o_ss = _head_sums(o * o)
        yield
        o = o * lax.rsqrt(o_ss * (1.0 / HEAD_DIM) + RMS_EPS) * hgrn_nw[...] * _silu(go_all)
        store_rows(mix_ref, g, A_WIDTH + B_WIDTH, o)

    def delayed(gen, rounds):
        for _ in range(rounds):
            yield
        yield from gen

    mixers = []
    for g in range(group):
        mixers += [delayed(m, g * GROUP_STAGGER) for m in (gdn_mixer(g), rwkv_mixer(g), hgrn_mixer(g))]
    for _ in _lockstep(mixers):
        pass

    if single:
        store_states()
    else:
        pl.when(t == last_t)(store_states)


def _mixers(layer, p, vfirst, states_in, state_layer, acc, lp, rows, single, group):
    bsz, t, _ = p.shape
    t_blk = 1 if single else rows
    nt = t // t_blk
    assert nt * t_blk == t and bsz % group == 0 and (not single or t == 1)

    def per_bt(width):
        return pl.BlockSpec((group, t_blk, width), lambda b, t_: (b, t_, 0))

    def per_b(shape, lyr):
        nd = len(shape)
        return pl.BlockSpec((None, group) + shape, lambda b, t_: (lyr, b) + (0,) * nd)

    def const(arr):
        nd = arr.ndim
        return pl.BlockSpec(arr.shape, lambda b, t_: (0,) * nd)

    inputs = [p]
    in_specs = [per_bt(P_COLS)]
    if layer > 0:
        inputs.append(vfirst)
        in_specs.append(per_bt(B_WIDTH))
    inputs += list(states_in)
    in_specs += [per_b(s, state_layer) for s in STATE_SHAPES]
    names = ["conv_w", "alog", "dtb", "gdn_nw", "mu", "w0", "w2", "a0", "a2", "g2"]
    if layer > 0:
        names += ["v0", "v1", "v2"]
    names += ["kk_w", "ka_w", "rk_w", "lnw", "lnb", "hlb", "hgrn_nw"]
    for nm in names:
        inputs.append(lp[nm])
        in_specs.append(const(lp[nm]))
    acc_first = len(inputs)
    inputs += list(acc)
    in_specs += [pl.BlockSpec(memory_space=pl.ANY) for _ in acc]

    out_shape = [jax.ShapeDtypeStruct((bsz, t, D_MODEL), BF)]
    out_specs = [per_bt(D_MODEL)]
    if layer == 0:
        out_shape.append(jax.ShapeDtypeStruct((bsz, t, B_WIDTH), F32))
        out_specs.append(per_bt(B_WIDTH))
    state_first = len(out_shape)
    for s, a in zip(STATE_SHAPES, acc):
        out_shape.append(jax.ShapeDtypeStruct(a.shape, F32))
        out_specs.append(per_b(s, layer))
    aliases = {acc_first + i: state_first + i for i in range(len(acc))}

    outs = pl.pallas_call(
        functools.partial(_mixer_kernel, layer, rows, single, group),
        grid=(bsz // group, nt),
        in_specs=in_specs,
        out_specs=out_specs,
        out_shape=out_shape,
        input_output_aliases=aliases,
        scratch_shapes=[pltpu.VMEM((group, CARRY_ROWS + rows, 3 * A_WIDTH), F32),
                        pltpu.VMEM((group, CARRY_ROWS + rows, B_COLS), F32),
                        pltpu.VMEM((group, H_A // 2, PAIR, PAIR), F32),
                        pltpu.VMEM((group, H_B // 2, PAIR, PAIR), F32),
                        pltpu.VMEM((group, H_C // 2, PAIR, PAIR), F32)],
        compiler_params=pltpu.CompilerParams(dimension_semantics=("arbitrary", "arbitrary")),
        name=f"mixers_l{layer}_r{rows}",
    )(*inputs)
    mix = outs[0]
    if layer == 0:
        vfirst = outs[1]
    return mix, vfirst, tuple(outs[state_first:])


def _trunk(x, states_in, layered_states, W):
    bsz, t, _ = x.shape
    if t == 1:
        rows, single, group = DECODE_ROWS, True, DECODE_GROUP
    else:
        assert t % PROMPT_CHUNK == 0
        rows, single, group = PROMPT_CHUNK, False, PROMPT_GROUP
    n = bsz * t
    xf = x.reshape(n, D_MODEL)
    fill = jnp.minimum(jnp.abs(xf[0, 0]), 0.0)
    acc = tuple(jnp.full((DEPTH, bsz) + s, fill, F32) for s in STATE_SHAPES)
    vfirst = None
    for l in range(DEPTH):
        lp = W["layers"][l]
        p = _inproj(xf, lp["norm_mix"], W["w_in"], l).reshape(bsz, t, P_COLS)
        mix, vfirst, acc = _mixers(l, p, vfirst, states_in, l if layered_states else 0, acc, lp,
                                   rows, single, group)
        xf = _ffn(xf, mix.reshape(n, D_MODEL), W["w_out"], lp["norm_ffn"], W["w_gate"], W["w_up"],
                  W["w_down"], W["norm_final"], l, final=(l == DEPTH - 1))
    y = xf.reshape(bsz, t, D_MODEL)
    gdn, conv, rwkv, shift, hgrn = acc
    return y, (gdn, conv, rwkv, shift[:, :, 0, :], hgrn)


def _prepare_weights(norm_mix, w_in, w_out, gdn_conv_w, gdn_a_log, gdn_dt_bias, gdn_norm_w,
                     rwkv_mu, rwkv_w0, rwkv_w2, rwkv_a0, rwkv_a2, rwkv_g2, rwkv_v0, rwkv_v1, rwkv_v2,
                     rwkv_k_k, rwkv_k_a, rwkv_r_k, rwkv_ln_w, rwkv_ln_b,
                     hgrn_lower_bounds, hgrn_norm_w, norm_ffn, w_gate, w_up, w_down, norm_final):
    depth = w_in.shape[0]
    ba_cols = w_in[:, :, 4 * A_WIDTH:A_COLS]
    pad = jnp.zeros((depth, D_MODEL, LANES - 2 * H_A), w_in.dtype)
    w_in_r = jnp.concatenate([w_in[:, :, :4 * A_WIDTH], w_in[:, :, A_COLS:], ba_cols, pad], axis=-1).astype(BF)

    def gate_row(v):
        z_lo = jnp.zeros((depth, GATE_LANE_A), F32)
        z_hi = jnp.zeros((depth, LANES - GATE_LANE_A - H_A), F32)
        return jnp.concatenate([z_lo, v.astype(F32), z_hi], axis=-1)[:, None, :]

    alog = gate_row(gdn_a_log)
    dtb = gate_row(gdn_dt_bias)
    layers = []
    for l in range(depth):
        lp = {
            "norm_mix": norm_mix[l][None], "norm_ffn": norm_ffn[l][None],
            "conv_w": gdn_conv_w[l], "alog": alog[l], "dtb": dtb[l],
            "gdn_nw": jnp.tile(gdn_norm_w[l], H_A)[None],
            "mu": rwkv_mu[l][None], "w0": rwkv_w0[l][None], "w2": rwkv_w2[l], "a0": rwkv_a0[l][None],
            "a2": rwkv_a2[l], "g2": rwkv_g2[l],
            "kk_w": rwkv_k_k[l][None], "ka_w": rwkv_k_a[l][None], "rk_w": rwkv_r_k[l].reshape(1, B_WIDTH),
            "lnw": rwkv_ln_w[l][None], "lnb": rwkv_ln_b[l][None],
            "hlb": hgrn_lower_bounds, "hgrn_nw": jnp.tile(hgrn_norm_w[l], H_C)[None],
        }
        if l > 0:
            lp.update(v0=rwkv_v0[l - 1][None], v1=rwkv_v1[l - 1], v2=rwkv_v2[l - 1])
        layers.append(lp)
    return {"layers": layers, "norm_final": norm_final[None], "w_in": w_in_r, "w_out": w_out.astype(BF),
            "w_gate": w_gate.astype(BF), "w_up": w_up.astype(BF), "w_down": w_down.astype(BF)}


def kernel(x_prompt, x_sample, state_gdn, state_gdn_conv, state_rwkv, state_rwkv_shift, state_hgrn, norm_mix, w_in, w_out, gdn_conv_w, gdn_a_log, gdn_dt_bias, gdn_norm_w, rwkv_mu, rwkv_w0, rwkv_w2, rwkv_a0, rwkv_a2, rwkv_g2, rwkv_v0, rwkv_v1, rwkv_v2, rwkv_k_k, rwkv_k_a, rwkv_r_k, rwkv_ln_w, rwkv_ln_b, hgrn_lower_bounds, hgrn_norm_w, norm_ffn, w_gate, w_up, w_down, norm_final):
    W = _prepare_weights(norm_mix, w_in, w_out, gdn_conv_w, gdn_a_log, gdn_dt_bias, gdn_norm_w,
                         rwkv_mu, rwkv_w0, rwkv_w2, rwkv_a0, rwkv_a2, rwkv_g2, rwkv_v0, rwkv_v1, rwkv_v2,
                         rwkv_k_k, rwkv_k_a, rwkv_r_k, rwkv_ln_w, rwkv_ln_b,
                         hgrn_lower_bounds, hgrn_norm_w, norm_ffn, w_gate, w_up, w_down, norm_final)
    b = x_prompt.shape[0]
    zero_states = tuple(jnp.zeros((1, b) + s, F32) for s in STATE_SHAPES)
    y_prompt, (gdn_p, conv_p, rwkv_p, shift_p, hgrn_p) = _trunk(x_prompt, zero_states, False, W)
    sample_states = (state_gdn, state_gdn_conv, state_rwkv, state_rwkv_shift[:, :, None, :], state_hgrn)
    y_sample, (gdn_s, conv_s, rwkv_s, shift_s, hgrn_s) = _trunk(x_sample, sample_states, True, W)
    return (y_prompt, y_sample, gdn_p, gdn_s, conv_p, conv_s, rwkv_p, rwkv_s, shift_p, shift_s, hgrn_p, hgrn_s)
```

```python
import functools

import jax
import jax.numpy as jnp
from jax import lax
from jax.experimental import pallas as pl
from jax.experimental.pallas import tpu as pltpu

D_MODEL = 1024
DEPTH = 4
HEAD_DIM = 64
A_WIDTH = 384
B_WIDTH = 384
C_WIDTH = 256
H_A = A_WIDTH // HEAD_DIM
H_B = B_WIDTH // HEAD_DIM
H_C = C_WIDTH // HEAD_DIM
CONV_W = 4
W_LORA = 64
A_LORA = 64
G_LORA = 128
V_LORA = 32
A_COLS = 4 * A_WIDTH + 2 * H_A
B_COLS = 3 * B_WIDTH + W_LORA + A_LORA + G_LORA
C_COLS = 4 * C_WIDTH
D_FF = 2816
FFN_CHUNK = 256
RMS_EPS = 1e-6
RWKV_LN_EPS = 64e-5
RWKV_DECAY_SCALE = 0.6065306597126334

LANES = 128
PAIR = 2 * HEAD_DIM
OFF_QKV = 0
OFF_Z = 3 * A_WIDTH
OFF_B = 4 * A_WIDTH
OFF_C = OFF_B + B_COLS
OFF_BA = OFF_C + C_COLS
P_COLS = OFF_BA + LANES
GATE_LANE_BETA = 0
GATE_LANE_A = H_A

PROMPT_CHUNK = 64
PROMPT_GROUP = 2
GROUP_STAGGER = 0
DECODE_ROWS = 8
DECODE_GROUP = 4
CARRY_ROWS = 8

STATE_SHAPES = ((H_A, HEAD_DIM, HEAD_DIM), (CONV_W - 1, 3 * A_WIDTH), (H_B, HEAD_DIM, HEAD_DIM),
                (1, B_COLS), (H_C, HEAD_DIM, HEAD_DIM))

BF = jnp.bfloat16
F32 = jnp.float32


def _mm(a, b):
    return jnp.dot(a.astype(BF), b.astype(BF), preferred_element_type=F32)


def _mm_nt(a, b):
    return lax.dot_general(a.astype(BF), b.astype(BF), (((1,), (1,)), ((), ())),
                           preferred_element_type=F32)


def _mm_tn(a, b):
    return lax.dot_general(a.astype(BF), b.astype(BF), (((0,), (0,)), ((), ())),
                           preferred_element_type=F32)


def _split3(x):
    x1 = x.astype(BF)
    r1 = x - x1.astype(F32)
    x2 = r1.astype(BF)
    x3 = (r1 - x2.astype(F32)).astype(BF)
    return x1, x2, x3


def _split2(x):
    x1 = x.astype(BF)
    x2 = (x - x1.astype(F32)).astype(BF)
    return x1, x2


def _mm_sel(sel, x, parts=_split2):
    s = sel.astype(BF)
    out = None
    for xp in parts(x):
        d = jnp.dot(s, xp, preferred_element_type=F32)
        out = d if out is None else out + d
    return out


def _mm_sel_r(x, sel, parts=_split2):
    s = sel.astype(BF)
    out = None
    for xp in parts(x):
        d = jnp.dot(xp, s, preferred_element_type=F32)
        out = d if out is None else out + d
    return out


def _colsum_bcast(x, parts=_split2):
    ones = jnp.ones((x.shape[0], LANES), BF)
    dn = (((0,), (0,)), ((), ()))
    out = None
    for xp in parts(x):
        d = lax.dot_general(xp, ones, dn, preferred_element_type=F32)
        out = d if out is None else out + d
    return out


def _head_sums(x):
    on_lo = jnp.where(lax.broadcasted_iota(jnp.int32, (1, PAIR), 1) < HEAD_DIM, 1.0, 0.0)
    on_hi = 1.0 - on_lo
    tiles = []
    for j in range(x.shape[-1] // PAIR):
        t = x[:, j * PAIR:(j + 1) * PAIR]
        s_lo = jnp.sum(t * on_lo, -1, keepdims=True)
        s_hi = jnp.sum(t * on_hi, -1, keepdims=True)
        tiles.append(s_lo * on_lo + s_hi * on_hi)
    return jnp.concatenate(tiles, axis=1)


def _sigmoid(x):
    return 1.0 / (1.0 + jnp.exp(-x))


def _silu(x):
    return x * _sigmoid(x)


def _softplus(x):
    return jnp.maximum(x, 0.0) + jnp.log(1.0 + jnp.exp(-jnp.abs(x)))


def _rmsnorm(x, w):
    return x * lax.rsqrt(jnp.mean(x * x, -1, keepdims=True) + RMS_EPS) * w


def _lockstep(gens):
    gens = list(gens)
    while gens:
        alive = []
        for g in gens:
            try:
                next(g)
                alive.append(g)
            except StopIteration:
                pass
        gens = alive
        if gens:
            yield


def _inproj_kernel(x_ref, nw_ref, w_ref, o_ref):
    h = _rmsnorm(x_ref[...], nw_ref[...]).astype(BF)
    step = 512
    for c in range(0, P_COLS, step):
        o_ref[:, c:c + step] = jnp.dot(h, w_ref[:, c:c + step], preferred_element_type=F32)


def _inproj(x, norm_w, w, layer):
    n = x.shape[0]
    tm = min(512, n)
    assert n % tm == 0
    return pl.pallas_call(
        _inproj_kernel,
        grid=(n // tm,),
        in_specs=[
            pl.BlockSpec((tm, D_MODEL), lambda i: (i, 0)),
            pl.BlockSpec((1, D_MODEL), lambda i: (0, 0)),
            pl.BlockSpec((None, D_MODEL, P_COLS), lambda i: (layer, 0, 0)),
        ],
        out_specs=pl.BlockSpec((tm, P_COLS), lambda i: (i, 0)),
        out_shape=jax.ShapeDtypeStruct((n, P_COLS), F32),
        compiler_params=pltpu.CompilerParams(dimension_semantics=("arbitrary",)),
        name="inproj",
    )(x, norm_w, w)


def _ffn_kernel(final, x_ref, mix_ref, wo_ref, nf_ref, wg_ref, wu_ref, wd_ref, nfin_ref, o_ref):
    x = x_ref[...] + jnp.dot(mix_ref[...], wo_ref[...], preferred_element_type=F32)
    h = _rmsnorm(x, nf_ref[...]).astype(BF)
    half = FFN_CHUNK
    acc = x
    for c in range(0, D_FF, half):
        gate = jnp.dot(h, wg_ref[:, c:c + half], preferred_element_type=F32)
        up = jnp.dot(h, wu_ref[:, c:c + half], preferred_element_type=F32)
        u = (_silu(gate) * up).astype(BF)
        acc = acc + jnp.dot(u, wd_ref[c:c + half, :], preferred_element_type=F32)
    if final:
        acc = _rmsnorm(acc, nfin_ref[...])
    o_ref[...] = acc


def _ffn(x, mix, w_out, norm_ffn, w_gate, w_up, w_down, norm_final, layer, final):
    n = x.shape[0]
    tm = min(512, n)
    assert n % tm == 0
    const = lambda i: (0, 0)
    of_layer = lambda i: (layer, 0, 0)
    once = pl.Buffered(1)
    return pl.pallas_call(
        functools.partial(_ffn_kernel, final),
        grid=(n // tm,),
        in_specs=[
            pl.BlockSpec((tm, D_MODEL), lambda i: (i, 0)),
            pl.BlockSpec((tm, D_MODEL), lambda i: (i, 0)),
            pl.BlockSpec((None, D_MODEL, D_MODEL), of_layer, pipeline_mode=once),
            pl.BlockSpec((1, D_MODEL), const),
            pl.BlockSpec((None, D_MODEL, D_FF), of_layer, pipeline_mode=once),
            pl.BlockSpec((None, D_MODEL, D_FF), of_layer, pipeline_mode=once),
            pl.BlockSpec((None, D_FF, D_MODEL), of_layer, pipeline_mode=once),
            pl.BlockSpec((1, D_MODEL), const),
        ],
        out_specs=pl.BlockSpec((tm, D_MODEL), lambda i: (i, 0)),
        out_shape=jax.ShapeDtypeStruct((n, D_MODEL), F32),
        compiler_params=pltpu.CompilerParams(dimension_semantics=("arbitrary",),
                                             vmem_limit_bytes=56 * 1024 * 1024),
        name="ffn",
    )(x, mix, w_out, norm_ffn, w_gate, w_up, w_down, norm_final)


def _mixer_kernel(layer, rows, single, group, *refs):
    C = rows
    assert single or C == HEAD_DIM
    it = iter(refs)
    p_ref = next(it)
    vf_in_ref = next(it) if layer > 0 else None
    gdn_s0, conv0, rwkv_s0, shift0, hgrn_s0 = (next(it) for _ in range(5))
    conv_w, alog, dtb, gdn_nw = (next(it) for _ in range(4))
    mu, w0, w2, a0, a2, g2 = (next(it) for _ in range(6))
    if layer > 0:
        v0, v1, v2 = (next(it) for _ in range(3))
    kk_w, ka_w, rk_w, lnw, lnb = (next(it) for _ in range(5))
    hlb, hgrn_nw = (next(it) for _ in range(2))
    for _ in STATE_SHAPES:
        next(it)
    mix_ref = next(it)
    vf_out_ref = next(it) if layer == 0 else None
    gdn_s, conv_s, rwkv_s, shift_s, hgrn_s = (next(it) for _ in range(5))
    xc_ref, pbx_ref, gdn_bd, rwkv_bd, hgrn_bd = (next(it) for _ in range(5))

    t = pl.program_id(1)
    last_t = pl.num_programs(1) - 1
    keep = CONV_W - 1
    c0 = CARRY_ROWS - keep
    s0 = CARRY_ROWS - 1
    t_valid = 1 if single else C
    lo, hi = slice(0, HEAD_DIM), slice(HEAD_DIM, PAIR)
    state_io = ((gdn_s0, gdn_s, gdn_bd, H_A), (rwkv_s0, rwkv_s, rwkv_bd, H_B), (hgrn_s0, hgrn_s, hgrn_bd, H_C))

    def load_states():
        for src, _, bd, n_heads in state_io:
            bd[...] = jnp.zeros(bd.shape, F32)
            for g in range(group):
                for j in range(n_heads // 2):
                    bd[g, j, lo, lo] = src[g, 2 * j]
                    bd[g, j, hi, hi] = src[g, 2 * j + 1]
        xc_ref[:, c0:CARRY_ROWS, :] = conv0[...]
        pbx_ref[:, s0:CARRY_ROWS, :] = shift0[...]

    def store_states():
        for _, dst, bd, n_heads in state_io:
            for g in range(group):
                for j in range(n_heads // 2):
                    dst[g, 2 * j] = bd[g, j, lo, lo]
                    dst[g, 2 * j + 1] = bd[g, j, hi, hi]

    if single:
        load_states()
    else:
        pl.when(t == 0)(load_states)

    row = lax.broadcasted_iota(jnp.int32, (C, C), 0)
    col = lax.broadcasted_iota(jnp.int32, (C, C), 1)
    tri = jnp.where(col <= row, 1.0, 0.0).astype(F32)
    row2 = lax.broadcasted_iota(jnp.int32, (C, 2 * C), 0)
    col2 = lax.broadcasted_iota(jnp.int32, (C, 2 * C), 1) & (C - 1)
    def as01(cond):
        return jnp.where(cond, 1.0, 0.0).astype(F32)

    causal2 = as01(col2 <= row2)
    strict2 = as01(col2 < row2)
    mask_ar2 = jnp.concatenate([strict2, causal2], axis=0)
    lane_lo = lax.broadcasted_iota(jnp.int32, (1, PAIR), 1) < HEAD_DIM
    on_lo = as01(lane_lo)
    on_hi = as01(~lane_lo)
    bd_mask = as01((lax.broadcasted_iota(jnp.int32, (PAIR, PAIR), 0) < HEAD_DIM)
                   == (lax.broadcasted_iota(jnp.int32, (PAIR, PAIR), 1) < HEAD_DIM))
    if single:
        valid = lax.broadcasted_iota(jnp.int32, (C, 1), 0) < 1

    def mask_rows(x):
        return jnp.where(valid, x, 0.0) if single else x

    def load_rows(ref, g, start, width):
        x = ref[g, :, start:start + width]
        return jnp.broadcast_to(x, (C, width)) if single else x

    def store_rows(ref, g, start, val):
        val = val[0:1] if single else val
        ref[g, :, start:start + val.shape[-1]] = val.astype(ref.dtype)

    def cumsum_rows(x, parts=_split2):
        if single:
            return jnp.broadcast_to(x[0:1], x.shape)
        return _mm_sel(tri, x, parts)

    def rbd(x):
        return jnp.concatenate([x * on_lo, x * on_hi], axis=0).astype(BF)

    def pair_cols(x):
        if C == HEAD_DIM:
            return x
        return jnp.concatenate([x[:, :C], x[:, HEAD_DIM:HEAD_DIM + C]], axis=1)

    def pair(x, j):
        return x[:, j * PAIR:(j + 1) * PAIR]

    sel_hi = as01(lax.broadcasted_iota(jnp.int32, (PAIR, HEAD_DIM), 0)
                  == lax.broadcasted_iota(jnp.int32, (PAIR, HEAD_DIM), 1) + HEAD_DIM).astype(BF)

    def stack_heads(x):
        return jnp.concatenate([x * on_lo, x * on_hi], axis=0)

    def combine(y):
        half = y.shape[0] // 2
        n = y.shape[1] // PAIR
        m_lo = jnp.concatenate([on_lo] * n, axis=1)
        m_hi = jnp.concatenate([on_hi] * n, axis=1)
        return y[:half] * m_lo + y[half:] * m_hi

    def to_stacked(m):
        m_bf = m.astype(BF)
        bot = jnp.dot(m_bf, sel_hi, preferred_element_type=F32).astype(BF)
        return jnp.concatenate([m_bf[:, :HEAD_DIM], bot], axis=0)

    def run_solve(ps, tiles):
        n = len(tiles)
        span = 1
        ps = ps.astype(BF)
        while True:
            x = jnp.concatenate(tiles, axis=1)
            x_hi = x.astype(BF)
            x_lo = (x - x_hi.astype(F32)).astype(BF)
            y = jnp.dot(ps, jnp.concatenate([x_hi, x_lo], axis=1), preferred_element_type=F32)
            span *= 2
            if span < C:
                ps_next = jnp.concatenate(
                    [jnp.dot(ps[:C], ps[:C], preferred_element_type=F32),
                     jnp.dot(ps[C:], ps[C:], preferred_element_type=F32)], axis=0).astype(BF)
            yield
            pr = combine(y[:, :n * PAIR] + y[:, n * PAIR:])
            tiles = [t + pr[:, i * PAIR:(i + 1) * PAIR] for i, t in enumerate(tiles)]
            if span >= C:
                return tiles
            ps = ps_next


    def gdn_mixer(g):
        xc_ref[g, CARRY_ROWS:CARRY_ROWS + C, :] = load_rows(p_ref, g, OFF_QKV, 3 * A_WIDTH)
        cw = conv_w[...]
        y = xc_ref[g, c0:c0 + C, :] * cw[0:1]
        for j in range(1, CONV_W):
            y = y + xc_ref[g, c0 + j:c0 + j + C, :] * cw[j:j + 1]
        new_buf = xc_ref[g, c0 + t_valid:c0 + t_valid + keep, :]
        conv_s[g] = new_buf
        xc_ref[g, c0:CARRY_ROWS, :] = new_buf
        qkv = _silu(y)
        q_all = qkv[:, 0:A_WIDTH]
        k_all = qkv[:, A_WIDTH:2 * A_WIDTH]
        v_all = mask_rows(qkv[:, 2 * A_WIDTH:3 * A_WIDTH])
        z_all = load_rows(p_ref, g, OFF_Z, A_WIDTH)
        ba = load_rows(p_ref, g, OFF_BA, LANES)
        beta_all = mask_rows(_sigmoid(ba))
        g_all = mask_rows(-jnp.exp(alog[...]) * _softplus(ba + dtb[...]))
        gcum = cumsum_rows(g_all, _split3)
        q_ss = _head_sums(q_all * q_all)
        k_ss = _head_sums(k_all * k_all)
        yield
        src_lane = lax.broadcasted_iota(jnp.int32, (LANES, A_WIDTH), 0)
        dst_head = lax.broadcasted_iota(jnp.int32, (LANES, A_WIDTH), 1) // HEAD_DIM
        beta_x = _mm_sel_r(beta_all, jnp.where(src_lane == dst_head + GATE_LANE_BETA, 1.0, 0.0))
        gcum_x = _mm_sel_r(gcum, jnp.where(src_lane == dst_head + GATE_LANE_A, 1.0, 0.0))
        q_all = q_all * lax.rsqrt(q_ss + 1e-6) * HEAD_DIM ** -0.5
        k_all = mask_rows(k_all * lax.rsqrt(k_ss + 1e-6))
        yield
        if not single:
            r6 = lax.broadcasted_iota(jnp.int32, (C, A_WIDTH), 0)
            c6 = lax.broadcasted_iota(jnp.int32, (C, A_WIDTH), 1) & (HEAD_DIM - 1)
            gr_x = _mm_sel(jnp.ones((C, C), F32), jnp.where(c6 == r6, gcum_x, 0.0))
            yield
            decay_x = jnp.exp(jnp.minimum(gcum_x - gr_x, 0.0)) * as01(c6 <= r6)
        egc_x = jnp.exp(gcum_x)
        gl_x = gcum_x[C - 1:C, :]
        kdec_x = k_all * jnp.exp(gl_x - gcum_x)
        u0_x = v_all * beta_x
        w0_x = k_all * (beta_x * egc_x)
        qe_x = q_all * egc_x
        o_tiles = [None] * (H_A // 2)

        def head_pair(j):
            k = pair(k_all, j)
            decay = causal2 if single else pair(decay_x, j)
            krb = rbd(k)
            if not single:
                kk = _mm_nt(k, krb)
            qk = _mm_nt(pair(q_all, j), krb)
            qs = _mm(pair(qe_x, j), gdn_bd[g, j])
            yield
            if single:
                u, w = pair(u0_x, j), pair(w0_x, j)
            else:
                neg_m = to_stacked(-(pair(beta_x, j) * kk * decay) * strict2)
                qkd = to_stacked(qk * decay)
                yield
                u, w = yield from run_solve(neg_m, [pair(u0_x, j), pair(w0_x, j)])
            ws = _mm(w, gdn_bd[g, j])
            yield
            v_new = u - ws
            if single:
                o2 = _mm(qk * decay, rbd(v_new))
            else:
                o2 = combine(jnp.dot(qkd, v_new.astype(BF), preferred_element_type=F32))
            ds = _mm_tn(pair(kdec_x, j), v_new)
            yield
            o_tiles[j] = qs + o2
            gdn_bd[g, j] = gdn_bd[g, j] * jnp.exp(pair(gl_x, j)) + ds * bd_mask

        yield from _lockstep([head_pair(j) for j in range(H_A // 2)])
        o = jnp.concatenate(o_tiles, axis=1)
        o_ss = _head_sums(o * o)
        yield
        o = o * lax.rsqrt(o_ss * (1.0 / HEAD_DIM) + RMS_EPS) * gdn_nw[...] * _silu(z_all)
        store_rows(mix_ref, g, 0, o)

    def rwkv_mixer(g):
        pb = load_rows(p_ref, g, OFF_B, B_COLS)
        pbx_ref[g, CARRY_ROWS:CARRY_ROWS + C, :] = pb
        prev = pbx_ref[g, s0:s0 + C, :]
        new_shift = pbx_ref[g, s0 + t_valid:s0 + t_valid + 1, :]
        shift_s[g] = new_shift
        pbx_ref[g, s0:CARRY_ROWS, :] = new_shift
        xs = pb + (prev - pb) * mu[...]
        r_all = xs[:, 0:B_WIDTH]
        k_all = xs[:, B_WIDTH:2 * B_WIDTH]
        v_all = xs[:, 2 * B_WIDTH:3 * B_WIDTH]
        o_w = 3 * B_WIDTH
        wd = xs[:, o_w:o_w + W_LORA]
        ad = xs[:, o_w + W_LORA:o_w + W_LORA + A_LORA]
        gd = xs[:, o_w + W_LORA + A_LORA:o_w + W_LORA + A_LORA + G_LORA]
        w_lora = _mm(jnp.tanh(wd), w2[...])
        a_lora = _mm(ad, a2[...])
        g_out = _mm(_sigmoid(gd), g2[...])
        kkx = k_all * kk_w[...]
        kk_ss = _head_sums(kkx * kkx)
        if layer == 0:
            store_rows(vf_out_ref, g, 0, v_all)
        else:
            v_lora = _mm(v_all, v1[...])
        yield
        log_decay = mask_rows(-(RWKV_DECAY_SCALE * _sigmoid(w0[...] + w_lora)))
        cum = cumsum_rows(log_decay)
        a_all = _sigmoid(a0[...] + a_lora)
        k_all = mask_rows(k_all * (1.0 + (a_all - 1.0) * ka_w[...]))
        rk_ss = _head_sums(r_all * k_all * rk_w[...])
        if layer > 0:
            v_mix = _mm(v_lora, v2[...])
        yield
        if layer > 0:
            mixv = _sigmoid(v0[...] + v_mix)
            v_all = v_all + (load_rows(vf_in_ref, g, 0, B_WIDTH) - v_all) * mixv
        v_all = mask_rows(v_all)
        kk = mask_rows(kkx * lax.rsqrt(kk_ss + 1e-6))
        bh = kk * a_all
        cl = cum[C - 1:C, :]
        einv = jnp.exp(-cum)
        ehat = jnp.exp(cl - cum)
        ar_x = jnp.concatenate([-kk * jnp.exp(cum - log_decay), r_all * jnp.exp(cum)], axis=0)
        bt_x = bh * einv
        kt_x = k_all * einv
        bk_x = jnp.concatenate([bh * ehat, k_all * ehat], axis=0)
        ecl_x = jnp.exp(cl)
        y_tiles = [None] * (H_B // 2)

        def head_pair(j):
            ar = pair(ar_x, j)
            v = pair(v_all, j)
            ars = _mm_nt(ar, rwkv_bd[g, j])
            gb = _mm_nt(ar, rbd(pair(bt_x, j)))
            gk = _mm_nt(ar, rbd(pair(kt_x, j)))
            yield
            gbm = gb * mask_ar2
            gkm = gk * mask_ar2
            lmv = _mm(gkm, rbd(v))
            if single:
                yield
                u = ars[:C]
            else:
                ps = to_stacked(gbm[:C])
                yield
                (u,) = yield from run_solve(ps, [ars[:C] + lmv[:C]])
            mrbu = _mm(gbm[C:], rbd(u))
            mrkv = lmv[C:]
            ds = _mm_tn(jnp.concatenate([u, v], axis=0), pair(bk_x, j))
            yield
            y_tiles[j] = ars[C:] + mrbu + mrkv
            rwkv_bd[g, j] = rwkv_bd[g, j] * pair(ecl_x, j) + ds * bd_mask

        yield from _lockstep([head_pair(j) for j in range(H_B // 2)])
        yh = jnp.concatenate(y_tiles, axis=1)
        mean = _head_sums(yh) * (1.0 / HEAD_DIM)
        yield
        yc = yh - mean
        var = _head_sums(yc * yc) * (1.0 / HEAD_DIM)
        yield
        yn = yc * lax.rsqrt(var + RWKV_LN_EPS) * lnw[...] + lnb[...]
        out = (yn + rk_ss * v_all) * g_out
        store_rows(mix_ref, g, A_WIDTH, out)

    def hgrn_mixer(g):
        pc = load_rows(p_ref, g, OFF_C, C_COLS)
        q_all = _silu(pc[:, 0:C_WIDTH]) * HEAD_DIM ** -0.5
        f_all = pc[:, C_WIDTH:2 * C_WIDTH]
        i_all = pc[:, 2 * C_WIDTH:3 * C_WIDTH]
        go_all = pc[:, 3 * C_WIDTH:4 * C_WIDTH]
        hl = hlb[...]
        hmax = jnp.max(hl, axis=0, keepdims=True)
        he = jnp.exp(hl - hmax)
        sm = he / jnp.sum(he, axis=0, keepdims=True)
        lb_acc = sm[0:1]
        for mrow in range(1, layer + 1):
            lb_acc = lb_acc + sm[mrow:mrow + 1]
        lb = lb_acc - sm[0:1]
        k_all = mask_rows((1.0 - lb) * _sigmoid(-f_all))
        log_f = jnp.log(1.0 - k_all)
        lsel = []
        pair_masks = []
        levels = []
        if not single:
            ls = 0
            while (1 << ls) < C:
                levels.append(ls)
                ls += 1
        for ls in levels:
            blk = row >> ls
            start = blk << ls
            odd = (blk & 1) == 1
            sel_q = odd & (col >= start) & (col <= row)
            sel_k = (~odd) & (col > row) & (col <= start + ((1 << ls) - 1))
            lsel.append(jnp.where(sel_q | sel_k, 1.0, 0.0).astype(F32))
            pair_masks.append(jnp.concatenate([as01(odd & ((col >> ls) == blk - 1))] * 2, axis=0))
        log_f_parts = _split2(log_f)
        shared = lambda _: log_f_parts
        bcum = cumsum_rows(log_f, shared)
        if levels:
            lvl_sums = _mm_sel(jnp.concatenate(lsel, axis=0), log_f, shared)
        col_sums = _colsum_bcast(log_f, shared)
        qk_ss = _head_sums(q_all * k_all)
        yield
        if levels:
            e_lvls = jnp.exp(lvl_sums)
        qe_x = q_all * jnp.exp(bcum)
        k_hat = k_all * jnp.exp(bcum[C - 1:C, :] - bcum)
        e_col = jnp.exp(col_sums)
        o_tiles = [None] * (H_C // 2)

        def head_pair(j):
            q = pair(q_all, j)
            k = pair(k_all, j)
            v = pair(i_all, j)
            parts = []
            for n in range(len(levels)):
                e = pair(e_lvls[n * C:(n + 1) * C], j)
                parts.append(_mm_nt(stack_heads(q * e), k * e))
            qs = _mm(pair(qe_x, j), hgrn_bd[g, j])
            ds = _mm_tn(pair(k_hat, j), v)
            yield
            o2 = pair(qk_ss, j) * v
            if levels:
                att = pair_masks[0] * parts[0]
                for n in range(1, len(levels)):
                    att = att + pair_masks[n] * parts[n]
                o2 = o2 + combine(_mm(att, v))
            yield
            o_tiles[j] = qs + o2
            hgrn_bd[g, j] = (e_col[j * PAIR:(j + 1) * PAIR, :] * hgrn_bd[g, j]
                             + ds * bd_mask)

        yield from _lockstep([head_pair(j) for j in range(H_C // 2)])
        o = jnp.concatenate(o_tiles, axis=1)
        o_ss = _head_sums(o * o)
        yield
        o = o * lax.rsqrt(o_ss * (1.0 / HEAD_DIM) + RMS_EPS) * hgrn_nw[...] * _silu(go_all)
        store_rows(mix_ref, g, A_WIDTH + B_WIDTH, o)

    def delayed(gen, rounds):
        for _ in range(rounds):
            yield
        yield from gen

    mixers = []
    for g in range(group):
        mixers += [delayed(m, g * GROUP_STAGGER) for m in (gdn_mixer(g), rwkv_mixer(g), hgrn_mixer(g))]
    for _ in _lockstep(mixers):
        pass

    if single:
        store_states()
    else:
        pl.when(t == last_t)(store_states)


def _mixers(layer, p, vfirst, states_in, state_layer, acc, lp, rows, single, group):
    bsz, t, _ = p.shape
    t_blk = 1 if single else rows
    nt = t // t_blk
    assert nt * t_blk == t and bsz % group == 0 and (not single or t == 1)

    def per_bt(width):
        return pl.BlockSpec((group, t_blk, width), lambda b, t_: (b, t_, 0))

    def per_b(shape, lyr):
        nd = len(shape)
        return pl.BlockSpec((None, group) + shape, lambda b, t_: (lyr, b) + (0,) * nd)

    def const(arr):
        nd = arr.ndim
        return pl.BlockSpec(arr.shape, lambda b, t_: (0,) * nd)

    inputs = [p]
    in_specs = [per_bt(P_COLS)]
    if layer > 0:
        inputs.append(vfirst)
        in_specs.append(per_bt(B_WIDTH))
    inputs += list(states_in)
    in_specs += [per_b(s, state_layer) for s in STATE_SHAPES]
    names = ["conv_w", "alog", "dtb", "gdn_nw", "mu", "w0", "w2", "a0", "a2", "g2"]
    if layer > 0:
        names += ["v0", "v1", "v2"]
    names += ["kk_w", "ka_w", "rk_w", "lnw", "lnb", "hlb", "hgrn_nw"]
    for nm in names:
        inputs.append(lp[nm])
        in_specs.append(const(lp[nm]))
    acc_first = len(inputs)
    inputs += list(acc)
    in_specs += [pl.BlockSpec(memory_space=pl.ANY) for _ in acc]

    out_shape = [jax.ShapeDtypeStruct((bsz, t, D_MODEL), BF)]
    out_specs = [per_bt(D_MODEL)]
    if layer == 0:
        out_shape.append(jax.ShapeDtypeStruct((bsz, t, B_WIDTH), F32))
        out_specs.append(per_bt(B_WIDTH))
    state_first = len(out_shape)
    for s, a in zip(STATE_SHAPES, acc):
        out_shape.append(jax.ShapeDtypeStruct(a.shape, F32))
        out_specs.append(per_b(s, layer))
    aliases = {acc_first + i: state_first + i for i in range(len(acc))}

    outs = pl.pallas_call(
        functools.partial(_mixer_kernel, layer, rows, single, group),
        grid=(bsz // group, nt),
        in_specs=in_specs,
        out_specs=out_specs,
        out_shape=out_shape,
        input_output_aliases=aliases,
        scratch_shapes=[pltpu.VMEM((group, CARRY_ROWS + rows, 3 * A_WIDTH), F32),
                        pltpu.VMEM((group, CARRY_ROWS + rows, B_COLS), F32),
                        pltpu.VMEM((group, H_A // 2, PAIR, PAIR), F32),
                        pltpu.VMEM((group, H_B // 2, PAIR, PAIR), F32),
                        pltpu.VMEM((group, H_C // 2, PAIR, PAIR), F32)],
        compiler_params=pltpu.CompilerParams(dimension_semantics=("arbitrary", "arbitrary")),
        name=f"mixers_l{layer}_r{rows}",
    )(*inputs)
    mix = outs[0]
    if layer == 0:
        vfirst = outs[1]
    return mix, vfirst, tuple(outs[state_first:])


def _trunk(x, states_in, layered_states, W):
    bsz, t, _ = x.shape
    if t == 1:
        rows, single, group = DECODE_ROWS, True, DECODE_GROUP
    else:
        assert t % PROMPT_CHUNK == 0
        rows, single, group = PROMPT_CHUNK, False, PROMPT_GROUP
    n = bsz * t
    xf = x.reshape(n, D_MODEL)
    fill = jnp.minimum(jnp.abs(xf[0, 0]), 0.0)
    acc = tuple(jnp.full((DEPTH, bsz) + s, fill, F32) for s in STATE_SHAPES)
    vfirst = None
    for l in range(DEPTH):
        lp = W["layers"][l]
        p = _inproj(xf, lp["norm_mix"], W["w_in"], l).reshape(bsz, t, P_COLS)
        mix, vfirst, acc = _mixers(l, p, vfirst, states_in, l if layered_states else 0, acc, lp,
                                   rows, single, group)
        xf = _ffn(xf, mix.reshape(n, D_MODEL), W["w_out"], lp["norm_ffn"], W["w_gate"], W["w_up"],
                  W["w_down"], W["norm_final"], l, final=(l == DEPTH - 1))
    y = xf.reshape(bsz, t, D_MODEL)
    gdn, conv, rwkv, shift, hgrn = acc
    return y, (gdn, conv, rwkv, shift[:, :, 0, :], hgrn)


def _prepare_weights(norm_mix, w_in, w_out, gdn_conv_w, gdn_a_log, gdn_dt_bias, gdn_norm_w,
                     rwkv_mu, rwkv_w0, rwkv_w2, rwkv_a0, rwkv_a2, rwkv_g2, rwkv_v0, rwkv_v1, rwkv_v2,
                     rwkv_k_k, rwkv_k_a, rwkv_r_k, rwkv_ln_w, rwkv_ln_b,
                     hgrn_lower_bounds, hgrn_norm_w, norm_ffn, w_gate, w_up, w_down, norm_final):
    depth = w_in.shape[0]
    ba_cols = w_in[:, :, 4 * A_WIDTH:A_COLS]
    pad = jnp.zeros((depth, D_MODEL, LANES - 2 * H_A), w_in.dtype)
    w_in_r = jnp.concatenate([w_in[:, :, :4 * A_WIDTH], w_in[:, :, A_COLS:], ba_cols, pad], axis=-1).astype(BF)

    def gate_row(v):
        z_lo = jnp.zeros((depth, GATE_LANE_A), F32)
        z_hi = jnp.zeros((depth, LANES - GATE_LANE_A - H_A), F32)
        return jnp.concatenate([z_lo, v.astype(F32), z_hi], axis=-1)[:, None, :]

    alog = gate_row(gdn_a_log)
    dtb = gate_row(gdn_dt_bias)
    layers = []
    for l in range(depth):
        lp = {
            "norm_mix": norm_mix[l][None], "norm_ffn": norm_ffn[l][None],
            "conv_w": gdn_conv_w[l], "alog": alog[l], "dtb": dtb[l],
            "gdn_nw": jnp.tile(gdn_norm_w[l], H_A)[None],
            "mu": rwkv_mu[l][None], "w0": rwkv_w0[l][None], "w2": rwkv_w2[l], "a0": rwkv_a0[l][None],
            "a2": rwkv_a2[l], "g2": rwkv_g2[l],
            "kk_w": rwkv_k_k[l][None], "ka_w": rwkv_k_a[l][None], "rk_w": rwkv_r_k[l].reshape(1, B_WIDTH),
            "lnw": rwkv_ln_w[l][None], "lnb": rwkv_ln_b[l][None],
            "hlb": hgrn_lower_bounds, "hgrn_nw": jnp.tile(hgrn_norm_w[l], H_C)[None],
        }
        if l > 0:
            lp.update(v0=rwkv_v0[l - 1][None], v1=rwkv_v1[l - 1], v2=rwkv_v2[l - 1])
        layers.append(lp)
    return {"layers": layers, "norm_final": norm_final[None], "w_in": w_in_r, "w_out": w_out.astype(BF),
            "w_gate": w_gate.astype(BF), "w_up": w_up.astype(BF), "w_down": w_down.astype(BF)}


def kernel(x_prompt, x_sample, state_gdn, state_gdn_conv, state_rwkv, state_rwkv_shift, state_hgrn, norm_mix, w_in, w_out, gdn_conv_w, gdn_a_log, gdn_dt_bias, gdn_norm_w, rwkv_mu, rwkv_w0, rwkv_w2, rwkv_a0, rwkv_a2, rwkv_g2, rwkv_v0, rwkv_v1, rwkv_v2, rwkv_k_k, rwkv_k_a, rwkv_r_k, rwkv_ln_w, rwkv_ln_b, hgrn_lower_bounds, hgrn_norm_w, norm_ffn, w_gate, w_up, w_down, norm_final):
    W = _prepare_weights(norm_mix, w_in, w_out, gdn_conv_w, gdn_a_log, gdn_dt_bias, gdn_norm_w,
                         rwkv_mu, rwkv_w0, rwkv_w2, rwkv_a0, rwkv_a2, rwkv_g2, rwkv_v0, rwkv_v1, rwkv_v2,
                         rwkv_k_k, rwkv_k_a, rwkv_r_k, rwkv_ln_w, rwkv_ln_b,
                         hgrn_lower_bounds, hgrn_norm_w, norm_ffn, w_gate, w_up, w_down, norm_final)
    b = x_prompt.shape[0]
    zero_states = tuple(jnp.zeros((1, b) + s, F32) for s in STATE_SHAPES)
    y_prompt, (gdn_p, conv_p, rwkv_p, shift_p, hgrn_p) = _trunk(x_prompt, zero_states, False, W)
    sample_states = (state_gdn, state_gdn_conv, state_rwkv, state_rwkv_shift[:, :, None, :], state_hgrn)
    y_sample, (gdn_s, conv_s, rwkv_s, shift_s, hgrn_s) = _trunk(x_sample, sample_states, True, W)
    return (y_prompt, y_sample, gdn_p, gdn_s, conv_p, conv_s, rwkv_p, rwkv_s, shift_p, shift_s, hgrn_p, hgrn_s)
```

```python
import functools

import jax
import jax.numpy as jnp
from jax import lax
from jax.experimental import pallas as pl
from jax.experimental.pallas import tpu as pltpu

D_MODEL = 1024
DEPTH = 4
HEAD_DIM = 64
A_WIDTH = 384
B_WIDTH = 384
C_WIDTH = 256
H_A = A_WIDTH // HEAD_DIM
H_B = B_WIDTH // HEAD_DIM
H_C = C_WIDTH // HEAD_DIM
CONV_W = 4
W_LORA = 64
A_LORA = 64
G_LORA = 128
V_LORA = 32
A_COLS = 4 * A_WIDTH + 2 * H_A
B_COLS = 3 * B_WIDTH + W_LORA + A_LORA + G_LORA
C_COLS = 4 * C_WIDTH
D_FF = 2816
FFN_CHUNK = 256
RMS_EPS = 1e-6
RWKV_LN_EPS = 64e-5
RWKV_DECAY_SCALE = 0.6065306597126334

LANES = 128
PAIR = 2 * HEAD_DIM
OFF_QKV = 0
OFF_Z = 3 * A_WIDTH
OFF_B = 4 * A_WIDTH
OFF_C = OFF_B + B_COLS
OFF_BA = OFF_C + C_COLS
P_COLS = OFF_BA + LANES
GATE_LANE_BETA = 0
GATE_LANE_A = H_A

PROMPT_CHUNK = 64
PROMPT_GROUP = 2
GROUP_STAGGER = 0
DECODE_ROWS = 8
DECODE_GROUP = 4
CARRY_ROWS = 8

STATE_SHAPES = ((H_A, HEAD_DIM, HEAD_DIM), (CONV_W - 1, 3 * A_WIDTH), (H_B, HEAD_DIM, HEAD_DIM),
                (1, B_COLS), (H_C, HEAD_DIM, HEAD_DIM))

BF = jnp.bfloat16
F32 = jnp.float32


def _mm(a, b):
    return jnp.dot(a.astype(BF), b.astype(BF), preferred_element_type=F32)


def _mm_nt(a, b):
    return lax.dot_general(a.astype(BF), b.astype(BF), (((1,), (1,)), ((), ())),
                           preferred_element_type=F32)


def _mm_tn(a, b):
    return lax.dot_general(a.astype(BF), b.astype(BF), (((0,), (0,)), ((), ())),
                           preferred_element_type=F32)


def _split3(x):
    x1 = x.astype(BF)
    r1 = x - x1.astype(F32)
    x2 = r1.astype(BF)
    x3 = (r1 - x2.astype(F32)).astype(BF)
    return x1, x2, x3


def _split2(x):
    x1 = x.astype(BF)
    x2 = (x - x1.astype(F32)).astype(BF)
    return x1, x2


def _mm_sel(sel, x, parts=_split2):
    s = sel.astype(BF)
    out = None
    for xp in parts(x):
        d = jnp.dot(s, xp, preferred_element_type=F32)
        out = d if out is None else out + d
    return out


def _mm_sel_r(x, sel, parts=_split2):
    s = sel.astype(BF)
    out = None
    for xp in parts(x):
        d = jnp.dot(xp, s, preferred_element_type=F32)
        out = d if out is None else out + d
    return out


def _colsum_bcast(x, parts=_split2):
    ones = jnp.ones((x.shape[0], LANES), BF)
    dn = (((0,), (0,)), ((), ()))
    out = None
    for xp in parts(x):
        d = lax.dot_general(xp, ones, dn, preferred_element_type=F32)
        out = d if out is None else out + d
    return out


def _head_sums(x):
    lane_lo = lax.broadcasted_iota(jnp.int32, (1, PAIR), 1) < HEAD_DIM
    tiles = []
    for j in range(x.shape[-1] // PAIR):
        t = x[:, j * PAIR:(j + 1) * PAIR]
        s_lo = jnp.sum(jnp.where(lane_lo, t, 0.0), -1, keepdims=True)
        s_hi = jnp.sum(jnp.where(lane_lo, 0.0, t), -1, keepdims=True)
        tiles.append(jnp.where(lane_lo, s_lo, s_hi))
    return jnp.concatenate(tiles, axis=1)


def _sigmoid(x):
    return 1.0 / (1.0 + jnp.exp(-x))


def _silu(x):
    return x * _sigmoid(x)


def _softplus(x):
    return jnp.maximum(x, 0.0) + jnp.log(1.0 + jnp.exp(-jnp.abs(x)))


def _rmsnorm(x, w):
    return x * lax.rsqrt(jnp.mean(x * x, -1, keepdims=True) + RMS_EPS) * w


def _lockstep(gens):
    gens = list(gens)
    while gens:
        alive = []
        for g in gens:
            try:
                next(g)
                alive.append(g)
            except StopIteration:
                pass
        gens = alive
        if gens:
            yield


def _inproj_kernel(x_ref, nw_ref, w_ref, o_ref):
    h = _rmsnorm(x_ref[...], nw_ref[...]).astype(BF)
    step = 512
    for c in range(0, P_COLS, step):
        o_ref[:, c:c + step] = jnp.dot(h, w_ref[:, c:c + step], preferred_element_type=F32)


def _inproj(x, norm_w, w, layer):
    n = x.shape[0]
    tm = min(512, n)
    assert n % tm == 0
    return pl.pallas_call(
        _inproj_kernel,
        grid=(n // tm,),
        in_specs=[
            pl.BlockSpec((tm, D_MODEL), lambda i: (i, 0)),
            pl.BlockSpec((1, D_MODEL), lambda i: (0, 0)),
            pl.BlockSpec((None, D_MODEL, P_COLS), lambda i: (layer, 0, 0)),
        ],
        out_specs=pl.BlockSpec((tm, P_COLS), lambda i: (i, 0)),
        out_shape=jax.ShapeDtypeStruct((n, P_COLS), F32),
        compiler_params=pltpu.CompilerParams(dimension_semantics=("arbitrary",)),
        name="inproj",
    )(x, norm_w, w)


def _ffn_kernel(final, x_ref, mix_ref, wo_ref, nf_ref, wg_ref, wu_ref, wd_ref, nfin_ref, o_ref):
    x = x_ref[...] + jnp.dot(mix_ref[...], wo_ref[...], preferred_element_type=F32)
    h = _rmsnorm(x, nf_ref[...]).astype(BF)
    half = FFN_CHUNK
    acc = x
    for c in range(0, D_FF, half):
        gate = jnp.dot(h, wg_ref[:, c:c + half], preferred_element_type=F32)
        up = jnp.dot(h, wu_ref[:, c:c + half], preferred_element_type=F32)
        u = (_silu(gate) * up).astype(BF)
        acc = acc + jnp.dot(u, wd_ref[c:c + half, :], preferred_element_type=F32)
    if final:
        acc = _rmsnorm(acc, nfin_ref[...])
    o_ref[...] = acc


def _ffn(x, mix, w_out, norm_ffn, w_gate, w_up, w_down, norm_final, layer, final):
    n = x.shape[0]
    tm = min(1024, n)
    assert n % tm == 0
    const = lambda i: (0, 0)
    of_layer = lambda i: (layer, 0, 0)
    once = pl.Buffered(1)
    return pl.pallas_call(
        functools.partial(_ffn_kernel, final),
        grid=(n // tm,),
        in_specs=[
            pl.BlockSpec((tm, D_MODEL), lambda i: (i, 0)),
            pl.BlockSpec((tm, D_MODEL), lambda i: (i, 0)),
            pl.BlockSpec((None, D_MODEL, D_MODEL), of_layer, pipeline_mode=once),
            pl.BlockSpec((1, D_MODEL), const),
            pl.BlockSpec((None, D_MODEL, D_FF), of_layer, pipeline_mode=once),
            pl.BlockSpec((None, D_MODEL, D_FF), of_layer, pipeline_mode=once),
            pl.BlockSpec((None, D_FF, D_MODEL), of_layer, pipeline_mode=once),
            pl.BlockSpec((1, D_MODEL), const),
        ],
        out_specs=pl.BlockSpec((tm, D_MODEL), lambda i: (i, 0)),
        out_shape=jax.ShapeDtypeStruct((n, D_MODEL), F32),
        compiler_params=pltpu.CompilerParams(dimension_semantics=("arbitrary",),
                                             vmem_limit_bytes=56 * 1024 * 1024),
        name="ffn",
    )(x, mix, w_out, norm_ffn, w_gate, w_up, w_down, norm_final)


def _mixer_kernel(layer, rows, single, group, *refs):
    C = rows
    assert single or C == HEAD_DIM
    it = iter(refs)
    p_ref = next(it)
    vf_in_ref = next(it) if layer > 0 else None
    gdn_s0, conv0, rwkv_s0, shift0, hgrn_s0 = (next(it) for _ in range(5))
    conv_w, alog, dtb, gdn_nw = (next(it) for _ in range(4))
    mu, w0, w2, a0, a2, g2 = (next(it) for _ in range(6))
    if layer > 0:
        v0, v1, v2 = (next(it) for _ in range(3))
    kk_w, ka_w, rk_w, lnw, lnb = (next(it) for _ in range(5))
    hlb, hgrn_nw = (next(it) for _ in range(2))
    for _ in STATE_SHAPES:
        next(it)
    mix_ref = next(it)
    vf_out_ref = next(it) if layer == 0 else None
    gdn_s, conv_s, rwkv_s, shift_s, hgrn_s = (next(it) for _ in range(5))
    xc_ref, pbx_ref, gdn_bd, rwkv_bd, hgrn_bd = (next(it) for _ in range(5))

    t = pl.program_id(1)
    last_t = pl.num_programs(1) - 1
    keep = CONV_W - 1
    c0 = CARRY_ROWS - keep
    s0 = CARRY_ROWS - 1
    t_valid = 1 if single else C
    lo, hi = slice(0, HEAD_DIM), slice(HEAD_DIM, PAIR)
    state_io = ((gdn_s0, gdn_s, gdn_bd, H_A), (rwkv_s0, rwkv_s, rwkv_bd, H_B), (hgrn_s0, hgrn_s, hgrn_bd, H_C))

    def load_states():
        for src, _, bd, n_heads in state_io:
            bd[...] = jnp.zeros(bd.shape, F32)
            for g in range(group):
                for j in range(n_heads // 2):
                    bd[g, j, lo, lo] = src[g, 2 * j]
                    bd[g, j, hi, hi] = src[g, 2 * j + 1]
        xc_ref[:, c0:CARRY_ROWS, :] = conv0[...]
        pbx_ref[:, s0:CARRY_ROWS, :] = shift0[...]

    def store_states():
        for _, dst, bd, n_heads in state_io:
            for g in range(group):
                for j in range(n_heads // 2):
                    dst[g, 2 * j] = bd[g, j, lo, lo]
                    dst[g, 2 * j + 1] = bd[g, j, hi, hi]

    if single:
        load_states()
    else:
        pl.when(t == 0)(load_states)

    row = lax.broadcasted_iota(jnp.int32, (C, C), 0)
    col = lax.broadcasted_iota(jnp.int32, (C, C), 1)
    tri = jnp.where(col <= row, 1.0, 0.0).astype(F32)
    row2 = lax.broadcasted_iota(jnp.int32, (C, 2 * C), 0)
    col2 = lax.broadcasted_iota(jnp.int32, (C, 2 * C), 1) & (C - 1)
    def as01(cond):
        return jnp.where(cond, 1.0, 0.0).astype(F32)

    causal2 = as01(col2 <= row2)
    strict2 = as01(col2 < row2)
    mask_ar2 = jnp.concatenate([strict2, causal2], axis=0)
    lane_lo = lax.broadcasted_iota(jnp.int32, (1, PAIR), 1) < HEAD_DIM
    on_lo = as01(lane_lo)
    on_hi = as01(~lane_lo)
    bd_mask = as01((lax.broadcasted_iota(jnp.int32, (PAIR, PAIR), 0) < HEAD_DIM)
                   == (lax.broadcasted_iota(jnp.int32, (PAIR, PAIR), 1) < HEAD_DIM))
    if single:
        valid = lax.broadcasted_iota(jnp.int32, (C, 1), 0) < 1

    def mask_rows(x):
        return jnp.where(valid, x, 0.0) if single else x

    def load_rows(ref, g, start, width):
        x = ref[g, :, start:start + width]
        return jnp.broadcast_to(x, (C, width)) if single else x

    def store_rows(ref, g, start, val):
        val = val[0:1] if single else val
        ref[g, :, start:start + val.shape[-1]] = val.astype(ref.dtype)

    def cumsum_rows(x, parts=_split2):
        if single:
            return jnp.broadcast_to(x[0:1], x.shape)
        return _mm_sel(tri, x, parts)

    def rbd(x):
        return jnp.concatenate([x * on_lo, x * on_hi], axis=0).astype(BF)

    def pair_cols(x):
        if C == HEAD_DIM:
            return x
        return jnp.concatenate([x[:, :C], x[:, HEAD_DIM:HEAD_DIM + C]], axis=1)

    def pair(x, j):
        return x[:, j * PAIR:(j + 1) * PAIR]

    sel_hi = as01(lax.broadcasted_iota(jnp.int32, (PAIR, HEAD_DIM), 0)
                  == lax.broadcasted_iota(jnp.int32, (PAIR, HEAD_DIM), 1) + HEAD_DIM).astype(BF)

    def stack_heads(x):
        return jnp.concatenate([x * on_lo, x * on_hi], axis=0)

    def combine(y):
        half = y.shape[0] // 2
        n = y.shape[1] // PAIR
        take_top = jnp.concatenate([lane_lo] * n, axis=1)
        return jnp.where(take_top, y[:half], y[half:])

    def to_stacked(m):
        m_bf = m.astype(BF)
        bot = jnp.dot(m_bf, sel_hi, preferred_element_type=F32).astype(BF)
        return jnp.concatenate([m_bf[:, :HEAD_DIM], bot], axis=0)

    def run_solve(ps, tiles):
        n = len(tiles)
        span = 1
        ps = ps.astype(BF)
        while True:
            x = jnp.concatenate(tiles, axis=1)
            x_hi = x.astype(BF)
            x_lo = (x - x_hi.astype(F32)).astype(BF)
            y = jnp.dot(ps, jnp.concatenate([x_hi, x_lo], axis=1), preferred_element_type=F32)
            span *= 2
            if span < C:
                ps_next = jnp.concatenate(
                    [jnp.dot(ps[:C], ps[:C], preferred_element_type=F32),
                     jnp.dot(ps[C:], ps[C:], preferred_element_type=F32)], axis=0).astype(BF)
            yield
            pr = combine(y[:, :n * PAIR] + y[:, n * PAIR:])
            tiles = [t + pr[:, i * PAIR:(i + 1) * PAIR] for i, t in enumerate(tiles)]
            if span >= C:
                return tiles
            ps = ps_next


    def gdn_mixer(g):
        xc_ref[g, CARRY_ROWS:CARRY_ROWS + C, :] = load_rows(p_ref, g, OFF_QKV, 3 * A_WIDTH)
        cw = conv_w[...]
        y = xc_ref[g, c0:c0 + C, :] * cw[0:1]
        for j in range(1, CONV_W):
            y = y + xc_ref[g, c0 + j:c0 + j + C, :] * cw[j:j + 1]
        new_buf = xc_ref[g, c0 + t_valid:c0 + t_valid + keep, :]
        conv_s[g] = new_buf
        xc_ref[g, c0:CARRY_ROWS, :] = new_buf
        qkv = _silu(y)
        q_all = qkv[:, 0:A_WIDTH]
        k_all = qkv[:, A_WIDTH:2 * A_WIDTH]
        v_all = mask_rows(qkv[:, 2 * A_WIDTH:3 * A_WIDTH])
        z_all = load_rows(p_ref, g, OFF_Z, A_WIDTH)
        ba = load_rows(p_ref, g, OFF_BA, LANES)
        beta_all = mask_rows(_sigmoid(ba))
        g_all = mask_rows(-jnp.exp(alog[...]) * _softplus(ba + dtb[...]))
        gcum = cumsum_rows(g_all, _split3)
        q_ss = _head_sums(q_all * q_all)
        k_ss = _head_sums(k_all * k_all)
        yield
        src_lane = lax.broadcasted_iota(jnp.int32, (LANES, A_WIDTH), 0)
        dst_head = lax.broadcasted_iota(jnp.int32, (LANES, A_WIDTH), 1) // HEAD_DIM
        beta_x = _mm_sel_r(beta_all, jnp.where(src_lane == dst_head + GATE_LANE_BETA, 1.0, 0.0))
        gcum_x = _mm_sel_r(gcum, jnp.where(src_lane == dst_head + GATE_LANE_A, 1.0, 0.0))
        q_all = q_all * lax.rsqrt(q_ss + 1e-6) * HEAD_DIM ** -0.5
        k_all = mask_rows(k_all * lax.rsqrt(k_ss + 1e-6))
        yield
        if not single:
            r6 = lax.broadcasted_iota(jnp.int32, (C, A_WIDTH), 0)
            c6 = lax.broadcasted_iota(jnp.int32, (C, A_WIDTH), 1) & (HEAD_DIM - 1)
            gr_x = _mm_sel(jnp.ones((C, C), F32), jnp.where(c6 == r6, gcum_x, 0.0))
            yield
            decay_x = jnp.exp(jnp.minimum(gcum_x - gr_x, 0.0)) * as01(c6 <= r6)
        egc_x = jnp.exp(gcum_x)
        gl_x = gcum_x[C - 1:C, :]
        kdec_x = k_all * jnp.exp(gl_x - gcum_x)
        u0_x = v_all * beta_x
        w0_x = k_all * (beta_x * egc_x)
        qe_x = q_all * egc_x
        o_tiles = [None] * (H_A // 2)

        def head_pair(j):
            k = pair(k_all, j)
            decay = causal2 if single else pair(decay_x, j)
            krb = rbd(k)
            if not single:
                kk = _mm_nt(k, krb)
            qk = _mm_nt(pair(q_all, j), krb)
            qs = _mm(pair(qe_x, j), gdn_bd[g, j])
            yield
            if single:
                u, w = pair(u0_x, j), pair(w0_x, j)
            else:
                neg_m = to_stacked(-(pair(beta_x, j) * kk * decay) * strict2)
                qkd = to_stacked(qk * decay)
                yield
                u, w = yield from run_solve(neg_m, [pair(u0_x, j), pair(w0_x, j)])
            ws = _mm(w, gdn_bd[g, j])
            yield
            v_new = u - ws
            if single:
                o2 = _mm(qk * decay, rbd(v_new))
            else:
                o2 = combine(jnp.dot(qkd, v_new.astype(BF), preferred_element_type=F32))
            ds = _mm_tn(pair(kdec_x, j), v_new)
            yield
            o_tiles[j] = qs + o2
            gdn_bd[g, j] = gdn_bd[g, j] * jnp.exp(pair(gl_x, j)) + ds * bd_mask

        yield from _lockstep([head_pair(j) for j in range(H_A // 2)])
        o = jnp.concatenate(o_tiles, axis=1)
        o_ss = _head_sums(o * o)
        yield
        o = o * lax.rsqrt(o_ss * (1.0 / HEAD_DIM) + RMS_EPS) * gdn_nw[...] * _silu(z_all)
        store_rows(mix_ref, g, 0, o)

    def rwkv_mixer(g):
        pb = load_rows(p_ref, g, OFF_B, B_COLS)
        pbx_ref[g, CARRY_ROWS:CARRY_ROWS + C, :] = pb
        prev = pbx_ref[g, s0:s0 + C, :]
        new_shift = pbx_ref[g, s0 + t_valid:s0 + t_valid + 1, :]
        shift_s[g] = new_shift
        pbx_ref[g, s0:CARRY_ROWS, :] = new_shift
        xs = pb + (prev - pb) * mu[...]
        r_all = xs[:, 0:B_WIDTH]
        k_all = xs[:, B_WIDTH:2 * B_WIDTH]
        v_all = xs[:, 2 * B_WIDTH:3 * B_WIDTH]
        o_w = 3 * B_WIDTH
        wd = xs[:, o_w:o_w + W_LORA]
        ad = xs[:, o_w + W_LORA:o_w + W_LORA + A_LORA]
        gd = xs[:, o_w + W_LORA + A_LORA:o_w + W_LORA + A_LORA + G_LORA]
        w_lora = _mm(jnp.tanh(wd), w2[...])
        a_lora = _mm(ad, a2[...])
        g_out = _mm(_sigmoid(gd), g2[...])
        kkx = k_all * kk_w[...]
        kk_ss = _head_sums(kkx * kkx)
        if layer == 0:
            store_rows(vf_out_ref, g, 0, v_all)
        else:
            v_lora = _mm(v_all, v1[...])
        yield
        log_decay = mask_rows(-(RWKV_DECAY_SCALE * _sigmoid(w0[...] + w_lora)))
        cum = cumsum_rows(log_decay)
        a_all = _sigmoid(a0[...] + a_lora)
        k_all = mask_rows(k_all * (1.0 + (a_all - 1.0) * ka_w[...]))
        rk_ss = _head_sums(r_all * k_all * rk_w[...])
        if layer > 0:
            v_mix = _mm(v_lora, v2[...])
        yield
        if layer > 0:
            mixv = _sigmoid(v0[...] + v_mix)
            v_all = v_all + (load_rows(vf_in_ref, g, 0, B_WIDTH) - v_all) * mixv
        v_all = mask_rows(v_all)
        kk = mask_rows(kkx * lax.rsqrt(kk_ss + 1e-6))
        bh = kk * a_all
        cl = cum[C - 1:C, :]
        einv = jnp.exp(-cum)
        ehat = jnp.exp(cl - cum)
        ar_x = jnp.concatenate([-kk * jnp.exp(cum - log_decay), r_all * jnp.exp(cum)], axis=0)
        bt_x = bh * einv
        kt_x = k_all * einv
        bk_x = jnp.concatenate([bh * ehat, k_all * ehat], axis=0)
        ecl_x = jnp.exp(cl)
        y_tiles = [None] * (H_B // 2)

        def head_pair(j):
            ar = pair(ar_x, j)
            v = pair(v_all, j)
            ars = _mm_nt(ar, rwkv_bd[g, j])
            gb = _mm_nt(ar, rbd(pair(bt_x, j)))
            gk = _mm_nt(ar, rbd(pair(kt_x, j)))
            yield
            gbm = gb * mask_ar2
            gkm = gk * mask_ar2
            lmv = _mm(gkm, rbd(v))
            if single:
                yield
                u = ars[:C]
            else:
                ps = to_stacked(gbm[:C])
                yield
                (u,) = yield from run_solve(ps, [ars[:C] + lmv[:C]])
            mrbu = _mm(gbm[C:], rbd(u))
            mrkv = lmv[C:]
            ds = _mm_tn(jnp.concatenate([u, v], axis=0), pair(bk_x, j))
            yield
            y_tiles[j] = ars[C:] + mrbu + mrkv
            rwkv_bd[g, j] = rwkv_bd[g, j] * pair(ecl_x, j) + ds * bd_mask

        yield from _lockstep([head_pair(j) for j in range(H_B // 2)])
        yh = jnp.concatenate(y_tiles, axis=1)
        mean = _head_sums(yh) * (1.0 / HEAD_DIM)
        yield
        yc = yh - mean
        var = _head_sums(yc * yc) * (1.0 / HEAD_DIM)
        yield
        yn = yc * lax.rsqrt(var + RWKV_LN_EPS) * lnw[...] + lnb[...]
        out = (yn + rk_ss * v_all) * g_out
        store_rows(mix_ref, g, A_WIDTH, out)

    def hgrn_mixer(g):
        pc = load_rows(p_ref, g, OFF_C, C_COLS)
        q_all = _silu(pc[:, 0:C_WIDTH]) * HEAD_DIM ** -0.5
        f_all = pc[:, C_WIDTH:2 * C_WIDTH]
        i_all = pc[:, 2 * C_WIDTH:3 * C_WIDTH]
        go_all = pc[:, 3 * C_WIDTH:4 * C_WIDTH]
        hl = hlb[...]
        hmax = jnp.max(hl, axis=0, keepdims=True)
        he = jnp.exp(hl - hmax)
        sm = he / jnp.sum(he, axis=0, keepdims=True)
        lb_acc = sm[0:1]
        for mrow in range(1, layer + 1):
            lb_acc = lb_acc + sm[mrow:mrow + 1]
        lb = lb_acc - sm[0:1]
        k_all = mask_rows((1.0 - lb) * _sigmoid(-f_all))
        log_f = jnp.log(1.0 - k_all)
        lsel = []
        pair_masks = []
        levels = []
        if not single:
            ls = 0
            while (1 << ls) < C:
                levels.append(ls)
                ls += 1
        for ls in levels:
            blk = row >> ls
            start = blk << ls
            odd = (blk & 1) == 1
            sel_q = odd & (col >= start) & (col <= row)
            sel_k = (~odd) & (col > row) & (col <= start + ((1 << ls) - 1))
            lsel.append(jnp.where(sel_q | sel_k, 1.0, 0.0).astype(F32))
            pair_masks.append(jnp.concatenate([as01(odd & ((col >> ls) == blk - 1))] * 2, axis=0))
        log_f_parts = _split2(log_f)
        shared = lambda _: log_f_parts
        bcum = cumsum_rows(log_f, shared)
        if levels:
            lvl_sums = _mm_sel(jnp.concatenate(lsel, axis=0), log_f, shared)
        col_sums = _colsum_bcast(log_f, shared)
        qk_ss = _head_sums(q_all * k_all)
        yield
        if levels:
            e_lvls = jnp.exp(lvl_sums)
        qe_x = q_all * jnp.exp(bcum)
        k_hat = k_all * jnp.exp(bcum[C - 1:C, :] - bcum)
        e_col = jnp.exp(col_sums)
        o_tiles = [None] * (H_C // 2)

        def head_pair(j):
            q = pair(q_all, j)
            k = pair(k_all, j)
            v = pair(i_all, j)
            parts = []
            for n in range(len(levels)):
                e = pair(e_lvls[n * C:(n + 1) * C], j)
                parts.append(_mm_nt(stack_heads(q * e), k * e))
            qs = _mm(pair(qe_x, j), hgrn_bd[g, j])
            ds = _mm_tn(pair(k_hat, j), v)
            yield
            o2 = pair(qk_ss, j) * v
            if levels:
                att = pair_masks[0] * parts[0]
                for n in range(1, len(levels)):
                    att = att + pair_masks[n] * parts[n]
                o2 = o2 + combine(_mm(att, v))
            yield
            o_tiles[j] = qs + o2
            hgrn_bd[g, j] = (e_col[j * PAIR:(j + 1) * PAIR, :] * hgrn_bd[g, j]
                             + ds * bd_mask)

        yield from _lockstep([head_pair(j) for j in range(H_C // 2)])
        o = jnp.concatenate(o_tiles, axis=1)
        o_ss = _head_sums(o * o)
        yield
        o = o * lax.rsqrt(o_ss * (1.0 / HEAD_DIM) + RMS_EPS) * hgrn_nw[...] * _silu(go_all)
        store_rows(mix_ref, g, A_WIDTH + B_WIDTH, o)

    def delayed(gen, rounds):
        for _ in range(rounds):
            yield
        yield from gen

    mixers = []
    for g in range(group):
        mixers += [delayed(m, g * GROUP_STAGGER) for m in (gdn_mixer(g), rwkv_mixer(g), hgrn_mixer(g))]
    for _ in _lockstep(mixers):
        pass

    if single:
        store_states()
    else:
        pl.when(t == last_t)(store_states)


def _mixers(layer, p, vfirst, states_in, state_layer, acc, lp, rows, single, group):
    bsz, t, _ = p.shape
    t_blk = 1 if single else rows
    nt = t // t_blk
    assert nt * t_blk == t and bsz % group == 0 and (not single or t == 1)

    def per_bt(width):
        return pl.BlockSpec((group, t_blk, width), lambda b, t_: (b, t_, 0))

    def per_b(shape, lyr):
        nd = len(shape)
        return pl.BlockSpec((None, group) + shape, lambda b, t_: (lyr, b) + (0,) * nd)

    def const(arr):
        nd = arr.ndim
        return pl.BlockSpec(arr.shape, lambda b, t_: (0,) * nd)

    inputs = [p]
    in_specs = [per_bt(P_COLS)]
    if layer > 0:
        inputs.append(vfirst)
        in_specs.append(per_bt(B_WIDTH))
    inputs += list(states_in)
    in_specs += [per_b(s, state_layer) for s in STATE_SHAPES]
    names = ["conv_w", "alog", "dtb", "gdn_nw", "mu", "w0", "w2", "a0", "a2", "g2"]
    if layer > 0:
        names += ["v0", "v1", "v2"]
    names += ["kk_w", "ka_w", "rk_w", "lnw", "lnb", "hlb", "hgrn_nw"]
    for nm in names:
        inputs.append(lp[nm])
        in_specs.append(const(lp[nm]))
    acc_first = len(inputs)
    inputs += list(acc)
    in_specs += [pl.BlockSpec(memory_space=pl.ANY) for _ in acc]

    out_shape = [jax.ShapeDtypeStruct((bsz, t, D_MODEL), BF)]
    out_specs = [per_bt(D_MODEL)]
    if layer == 0:
        out_shape.append(jax.ShapeDtypeStruct((bsz, t, B_WIDTH), F32))
        out_specs.append(per_bt(B_WIDTH))
    state_first = len(out_shape)
    for s, a in zip(STATE_SHAPES, acc):
        out_shape.append(jax.ShapeDtypeStruct(a.shape, F32))
        out_specs.append(per_b(s, layer))
    aliases = {acc_first + i: state_first + i for i in range(len(acc))}

    outs = pl.pallas_call(
        functools.partial(_mixer_kernel, layer, rows, single, group),
        grid=(bsz // group, nt),
        in_specs=in_specs,
        out_specs=out_specs,
        out_shape=out_shape,
        input_output_aliases=aliases,
        scratch_shapes=[pltpu.VMEM((group, CARRY_ROWS + rows, 3 * A_WIDTH), F32),
                        pltpu.VMEM((group, CARRY_ROWS + rows, B_COLS), F32),
                        pltpu.VMEM((group, H_A // 2, PAIR, PAIR), F32),
                        pltpu.VMEM((group, H_B // 2, PAIR, PAIR), F32),
                        pltpu.VMEM((group, H_C // 2, PAIR, PAIR), F32)],
        compiler_params=pltpu.CompilerParams(dimension_semantics=("arbitrary", "arbitrary")),
        name=f"mixers_l{layer}_r{rows}",
    )(*inputs)
    mix = outs[0]
    if layer == 0:
        vfirst = outs[1]
    return mix, vfirst, tuple(outs[state_first:])


def _trunk(x, states_in, layered_states, W):
    bsz, t, _ = x.shape
    if t == 1:
        rows, single, group = DECODE_ROWS, True, DECODE_GROUP
    else:
        assert t % PROMPT_CHUNK == 0
        rows, single, group = PROMPT_CHUNK, False, PROMPT_GROUP
    n = bsz * t
    xf = x.reshape(n, D_MODEL)
    fill = jnp.minimum(jnp.abs(xf[0, 0]), 0.0)
    acc = tuple(jnp.full((DEPTH, bsz) + s, fill, F32) for s in STATE_SHAPES)
    vfirst = None
    for l in range(DEPTH):
        lp = W["layers"][l]
        p = _inproj(xf, lp["norm_mix"], W["w_in"], l).reshape(bsz, t, P_COLS)
        mix, vfirst, acc = _mixers(l, p, vfirst, states_in, l if layered_states else 0, acc, lp,
                                   rows, single, group)
        xf = _ffn(xf, mix.reshape(n, D_MODEL), W["w_out"], lp["norm_ffn"], W["w_gate"], W["w_up"],
                  W["w_down"], W["norm_final"], l, final=(l == DEPTH - 1))
    y = xf.reshape(bsz, t, D_MODEL)
    gdn, conv, rwkv, shift, hgrn = acc
    return y, (gdn, conv, rwkv, shift[:, :, 0, :], hgrn)


def _prepare_weights(norm_mix, w_in, w_out, gdn_conv_w, gdn_a_log, gdn_dt_bias, gdn_norm_w,
                     rwkv_mu, rwkv_w0, rwkv_w2, rwkv_a0, rwkv_a2, rwkv_g2, rwkv_v0, rwkv_v1, rwkv_v2,
                     rwkv_k_k, rwkv_k_a, rwkv_r_k, rwkv_ln_w, rwkv_ln_b,
                     hgrn_lower_bounds, hgrn_norm_w, norm_ffn, w_gate, w_up, w_down, norm_final):
    depth = w_in.shape[0]
    ba_cols = w_in[:, :, 4 * A_WIDTH:A_COLS]
    pad = jnp.zeros((depth, D_MODEL, LANES - 2 * H_A), w_in.dtype)
    w_in_r = jnp.concatenate([w_in[:, :, :4 * A_WIDTH], w_in[:, :, A_COLS:], ba_cols, pad], axis=-1).astype(BF)

    def gate_row(v):
        z_lo = jnp.zeros((depth, GATE_LANE_A), F32)
        z_hi = jnp.zeros((depth, LANES - GATE_LANE_A - H_A), F32)
        return jnp.concatenate([z_lo, v.astype(F32), z_hi], axis=-1)[:, None, :]

    alog = gate_row(gdn_a_log)
    dtb = gate_row(gdn_dt_bias)
    layers = []
    for l in range(depth):
        lp = {
            "norm_mix": norm_mix[l][None], "norm_ffn": norm_ffn[l][None],
            "conv_w": gdn_conv_w[l], "alog": alog[l], "dtb": dtb[l],
            "gdn_nw": jnp.tile(gdn_norm_w[l], H_A)[None],
            "mu": rwkv_mu[l][None], "w0": rwkv_w0[l][None], "w2": rwkv_w2[l], "a0": rwkv_a0[l][None],
            "a2": rwkv_a2[l], "g2": rwkv_g2[l],
            "kk_w": rwkv_k_k[l][None], "ka_w": rwkv_k_a[l][None], "rk_w": rwkv_r_k[l].reshape(1, B_WIDTH),
            "lnw": rwkv_ln_w[l][None], "lnb": rwkv_ln_b[l][None],
            "hlb": hgrn_lower_bounds, "hgrn_nw": jnp.tile(hgrn_norm_w[l], H_C)[None],
        }
        if l > 0:
            lp.update(v0=rwkv_v0[l - 1][None], v1=rwkv_v1[l - 1], v2=rwkv_v2[l - 1])
        layers.append(lp)
    return {"layers": layers, "norm_final": norm_final[None], "w_in": w_in_r, "w_out": w_out.astype(BF),
            "w_gate": w_gate.astype(BF), "w_up": w_up.astype(BF), "w_down": w_down.astype(BF)}


def kernel(x_prompt, x_sample, state_gdn, state_gdn_conv, state_rwkv, state_rwkv_shift, state_hgrn, norm_mix, w_in, w_out, gdn_conv_w, gdn_a_log, gdn_dt_bias, gdn_norm_w, rwkv_mu, rwkv_w0, rwkv_w2, rwkv_a0, rwkv_a2, rwkv_g2, rwkv_v0, rwkv_v1, rwkv_v2, rwkv_k_k, rwkv_k_a, rwkv_r_k, rwkv_ln_w, rwkv_ln_b, hgrn_lower_bounds, hgrn_norm_w, norm_ffn, w_gate, w_up, w_down, norm_final):
    W = _prepare_weights(norm_mix, w_in, w_out, gdn_conv_w, gdn_a_log, gdn_dt_bias, gdn_norm_w,
                         rwkv_mu, rwkv_w0, rwkv_w2, rwkv_a0, rwkv_a2, rwkv_g2, rwkv_v0, rwkv_v1, rwkv_v2,
                         rwkv_k_k, rwkv_k_a, rwkv_r_k, rwkv_ln_w, rwkv_ln_b,
                         hgrn_lower_bounds, hgrn_norm_w, norm_ffn, w_gate, w_up, w_down, norm_final)
    b = x_prompt.shape[0]
    zero_states = tuple(jnp.zeros((1, b) + s, F32) for s in STATE_SHAPES)
    y_prompt, (gdn_p, conv_p, rwkv_p, shift_p, hgrn_p) = _trunk(x_prompt, zero_states, False, W)
    sample_states = (state_gdn, state_gdn_conv, state_rwkv, state_rwkv_shift[:, :, None, :], state_hgrn)
    y_sample, (gdn_s, conv_s, rwkv_s, shift_s, hgrn_s) = _trunk(x_sample, sample_states, True, W)
    return (y_prompt, y_sample, gdn_p, gdn_s, conv_p, conv_s, rwkv_p, rwkv_s, shift_p, shift_s, hgrn_p, hgrn_s)
```
